```python
import math
import jax, jax.numpy as jnp
from jax import lax
import numpy as np

D_MODEL = 4096
BATCH = 2
SEQ = 4096
DEPTH = 2

CHUNK = 64
RMS_EPS = 1e-6
GDN_HEAD_DIM = 128
GDN_HEADS = D_MODEL // GDN_HEAD_DIM
GDN_CONV = 4
GDN_IN = 4 * GDN_HEADS * GDN_HEAD_DIM + 2 * GDN_HEADS
DIFF_HEAD_DIM = 128
DIFF_HEADS = D_MODEL // (2 * DIFF_HEAD_DIM)
DIFF_QK = DIFF_HEADS * 2 * DIFF_HEAD_DIM
DIFF_V = DIFF_HEADS * 2 * DIFF_HEAD_DIM
Q_BLOCK = 128
FFN_HIDDEN = -(-8 * D_MODEL // (3 * 256)) * 256

kernel_name = "yoco_gated_deltanet_diff_attention_hybrid"


def rms_norm(x, gain):
    xf = x.astype(jnp.float32)
    y = xf * lax.rsqrt(jnp.mean(xf * xf, axis=-1, keepdims=True) + RMS_EPS)
    return (y * gain.astype(jnp.float32)).astype(x.dtype)


def l2_normalize(x):
    xf = x.astype(jnp.float32)
    return xf * lax.rsqrt(jnp.sum(xf * xf, axis=-1, keepdims=True) + 1e-6)


def causal_depthwise_conv(x, w):
    k_width, channels = w.shape
    return lax.conv_general_dilated(
        x, w[:, None, :].astype(x.dtype), window_strides=(1,), padding=[(k_width - 1, 0)],
        dimension_numbers=('NWC', 'WIO', 'NWC'), feature_group_count=channels)


def gated_delta_rule(q, k, v, g, beta):
    B, S, H, dk = q.shape
    dv = v.shape[-1]
    n = S // CHUNK
    f32 = jnp.float32

    def to_chunks(t):
        t = t.astype(f32).reshape((B, n, CHUNK, H) + t.shape[3:])
        return jnp.moveaxis(t, 3, 1)

    q = to_chunks(q) * (dk ** -0.5)
    k = to_chunks(k)
    v = to_chunks(v)
    beta = to_chunks(beta)
    g = jnp.cumsum(to_chunks(g), axis=-1)
    causal = jnp.tril(jnp.ones((CHUNK, CHUNK), bool))
    strict = jnp.tril(jnp.ones((CHUNK, CHUNK), bool), -1)
    diff = g[..., :, None] - g[..., None, :]
    decay = jnp.where(causal, jnp.exp(jnp.where(causal, diff, 0.0)), 0.0)
    kb = k * beta[..., None]
    a_low = jnp.where(strict, jnp.einsum('bhnid,bhnjd->bhnij', kb, k) * decay, 0.0)
    eye = jnp.eye(CHUNK, dtype=f32)
    t_inv = lax.linalg.triangular_solve(eye + a_low, jnp.broadcast_to(eye, a_low.shape),
                                        left_side=True, lower=True)
    u = t_inv @ (v * beta[..., None])
    w = t_inv @ (kb * jnp.exp(g)[..., None])
    intra = jnp.where(causal, jnp.einsum('bhnid,bhnjd->bhnij', q, k) * decay, 0.0)
    g_last = g[..., -1:]
    q_dec = q * jnp.exp(g)[..., None]
    k_dec = k * jnp.exp(g_last - g)[..., None]
    state_decay = jnp.exp(g_last[..., 0])
    xs = tuple(jnp.moveaxis(t, 2, 0) for t in (q_dec, k_dec, u, w, intra, state_decay))

    def step(state, inp):
        qd, kd, u_c, w_c, a_c, sd = inp
        v_new = u_c - w_c @ state
        o = qd @ state + a_c @ v_new
        state = state * sd[..., None, None] + jnp.swapaxes(kd, -1, -2) @ v_new
        return state, o

    state0 = jnp.zeros((B, H, dk, dv), f32)
    _, o = lax.scan(step, state0, xs)
    return o.transpose(1, 0, 3, 2, 4).reshape(B, S, H, dv)


def gdn_mixer(h, w_in, conv_w, a_log, dt_bias, out_gain, w_out):
    B, S, _ = h.shape
    H, d = GDN_HEADS, GDN_HEAD_DIM
    proj = h @ w_in
    qkv = proj[..., :3 * H * d]
    z = proj[..., 3 * H * d:4 * H * d]
    b = proj[..., 4 * H * d:4 * H * d + H]
    a = proj[..., 4 * H * d + H:]
    qkv = jax.nn.silu(causal_depthwise_conv(qkv, conv_w))
    q = l2_normalize(qkv[..., :H * d].reshape(B, S, H, d))
    k = l2_normalize(qkv[..., H * d:2 * H * d].reshape(B, S, H, d))
    v = qkv[..., 2 * H * d:].reshape(B, S, H, d)
    beta = jax.nn.sigmoid(b.astype(jnp.float32))
    g = -jnp.exp(a_log.astype(jnp.float32)) * jax.nn.softplus(a.astype(jnp.float32) + dt_bias.astype(jnp.float32))
    o = gated_delta_rule(q, k, v, g, beta)
    o = rms_norm(o, out_gain) * jax.nn.silu(z.reshape(B, S, H, d).astype(jnp.float32))
    return o.reshape(B, S, H * d).astype(h.dtype) @ w_out


def shared_kv(x, kv_norm, w_kv, k_norm):
    B, S, _ = x.shape
    kv = rms_norm(x, kv_norm) @ w_kv
    k = rms_norm(kv[..., :DIFF_QK].reshape(B, S, DIFF_HEADS, 2, DIFF_HEAD_DIM), k_norm)
    v = kv[..., DIFF_QK:].reshape(B, S, DIFF_HEADS, 2 * DIFF_HEAD_DIM)
    return k, v


def diff_attention(h, k, v, w_q, q_gain, lam_params, sub_gain, w_out, lam_init):
    B, S, _ = h.shape
    H, d = DIFF_HEADS, DIFF_HEAD_DIM
    q = rms_norm((h @ w_q).reshape(B, S, H, 2, d), q_gain)
    lp = lam_params.astype(jnp.float32)
    lam = jnp.exp(jnp.sum(lp[0] * lp[1])) - jnp.exp(jnp.sum(lp[2] * lp[3])) + lam_init
    nb = S // Q_BLOCK
    qb = q.reshape(B, nb, Q_BLOCK, H, 2, d).transpose(1, 0, 2, 3, 4, 5)
    key_chunk = jnp.arange(S) // CHUNK
    scale = d ** -0.5

    def block(args):
        q_blk, start = args
        s = jnp.einsum('bqhmd,bkhmd->bhmqk', q_blk, k, preferred_element_type=jnp.float32) * scale
        q_chunk = (start + jnp.arange(Q_BLOCK)) // CHUNK
        mask = key_chunk[None, :] <= q_chunk[:, None]
        p = jax.nn.softmax(jnp.where(mask, s, -jnp.inf), axis=-1)
        attn = p[:, :, 0] - lam * p[:, :, 1]
        return jnp.einsum('bhqk,bkhe->bqhe', attn.astype(v.dtype), v)

    o = lax.map(block, (qb, jnp.arange(nb) * Q_BLOCK))
    o = o.transpose(1, 0, 2, 3, 4).reshape(B, S, H, 2 * d)
    o = rms_norm(o, sub_gain) * (1.0 - lam_init)
    return o.reshape(B, S, H * 2 * d).astype(h.dtype) @ w_out


def swiglu(h, w_gate_up, w_down):
    gu = h @ w_gate_up
    gate, up = gu[..., :FFN_HIDDEN], gu[..., FFN_HIDDEN:]
    return (jax.nn.silu(gate) * up) @ w_down


def setup_inputs(seed: int = 0) -> dict:
    key = jax.random.key(seed)
    ks = iter(jax.random.split(key, 32))
    n_a = DEPTH // 2
    n_b = DEPTH - n_a
    D = D_MODEL

    def normal(shape, scale):
        return jax.random.normal(next(ks), shape, jnp.float32) * scale

    def gain(shape):
        return 1.0 + normal(shape, 0.02)

    x = normal((BATCH, SEQ, D), 1.0)
    a_norm = gain((n_a, D))
    a_w_in = normal((n_a, D, GDN_IN), D ** -0.5)
    a_conv = normal((n_a, GDN_CONV, 3 * GDN_HEADS * GDN_HEAD_DIM), GDN_CONV ** -0.5)
    a_A_log = jnp.log(jax.random.uniform(next(ks), (n_a, GDN_HEADS), jnp.float32, 1.0, 16.0))
    dt = jnp.exp(jax.random.uniform(next(ks), (n_a, GDN_HEADS), jnp.float32, math.log(1e-3), math.log(1e-1)))
    a_dt_bias = dt + jnp.log(-jnp.expm1(-dt))
    a_out_norm = gain((n_a, GDN_HEAD_DIM))
    a_w_out = normal((n_a, GDN_HEADS * GDN_HEAD_DIM, D), (GDN_HEADS * GDN_HEAD_DIM) ** -0.5)
    kv_norm = gain((D,))
    w_kv = normal((D, DIFF_QK + DIFF_V), D ** -0.5)
    k_norm = gain((DIFF_HEAD_DIM,))
    b_norm = gain((n_b, D))
    b_w_q = normal((n_b, D, DIFF_QK), D ** -0.5)
    b_q_norm = gain((n_b, DIFF_HEAD_DIM))
    b_lambda = normal((n_b, 4, DIFF_HEAD_DIM), 0.1)
    b_sub_norm = gain((n_b, 2 * DIFF_HEAD_DIM))
    b_w_out = normal((n_b, DIFF_V, D), DIFF_V ** -0.5)
    ffn_norm = gain((DEPTH, D))
    ffn_w_gate_up = normal((DEPTH, D, 2 * FFN_HIDDEN), D ** -0.5)
    ffn_w_down = normal((DEPTH, FFN_HIDDEN, D), FFN_HIDDEN ** -0.5)
    return {"x": x, "a_norm": a_norm, "a_w_in": a_w_in, "a_conv": a_conv, "a_A_log": a_A_log,
            "a_dt_bias": a_dt_bias, "a_out_norm": a_out_norm, "a_w_out": a_w_out,
            "kv_norm": kv_norm, "w_kv": w_kv, "k_norm": k_norm,
            "b_norm": b_norm, "b_w_q": b_w_q, "b_q_norm": b_q_norm, "b_lambda": b_lambda,
            "b_sub_norm": b_sub_norm, "b_w_out": b_w_out,
            "ffn_norm": ffn_norm, "ffn_w_gate_up": ffn_w_gate_up, "ffn_w_down": ffn_w_down}


def reference(x, a_norm, a_w_in, a_conv, a_A_log, a_dt_bias, a_out_norm, a_w_out,
              kv_norm, w_kv, k_norm, b_norm, b_w_q, b_q_norm, b_lambda, b_sub_norm, b_w_out,
              ffn_norm, ffn_w_gate_up, ffn_w_down):
    n_a = DEPTH // 2
    k_sh = None
    v_sh = None
    for layer in range(DEPTH):
        if layer < n_a:
            i = layer
            x = x + gdn_mixer(rms_norm(x, a_norm[i]), a_w_in[i], a_conv[i], a_A_log[i],
                              a_dt_bias[i], a_out_norm[i], a_w_out[i])
        else:
            if layer == n_a:
                k_sh, v_sh = shared_kv(x, kv_norm, w_kv, k_norm)
            j = layer - n_a
            lam_init = 0.8 - 0.6 * math.exp(-0.3 * layer)
            x = x + diff_attention(rms_norm(x, b_norm[j]), k_sh, v_sh, b_w_q[j], b_q_norm[j],
                                   b_lambda[j], b_sub_norm[j], b_w_out[j], lam_init)
        x = x + swiglu(rms_norm(x, ffn_norm[layer]), ffn_w_gate_up[layer], ffn_w_down[layer])
    return x
```

```python
import functools
import math

import jax
import jax.numpy as jnp
from jax import lax
from jax.experimental import pallas as pl
from jax.experimental.pallas import tpu as pltpu

F32 = jnp.float32
BF16 = jnp.bfloat16

RMS_EPS = 1e-6
L2_EPS = 1e-6
HEAD_DIM = 128
ATTN_CHUNK = 64
GDN_BLOCK = 128
GDN_CONV = 4
SUBLANES = 8
V7X_VMEM_CAP = 60000 * 1024


def _vmem_limit(block_bytes):
    return int(min(V7X_VMEM_CAP, block_bytes + (20 << 20)))


def _sigmoid(x):
    return 1.0 / (1.0 + jnp.exp(-x))


def _silu(x):
    return x * _sigmoid(x)


def _softplus(x):
    return jnp.maximum(x, 0.0) + jnp.log(1.0 + jnp.exp(-jnp.abs(x)))


def _dot(a, b):
    return jnp.dot(a, b, preferred_element_type=F32)


def _dot_nt(a, b):
    return lax.dot_general(a, b, (((1,), (1,)), ((), ())), preferred_element_type=F32)


def _dot_tn(a, b):
    return lax.dot_general(a, b, (((0,), (0,)), ((), ())), preferred_element_type=F32)


def _split3(x):
    hi = x.astype(BF16)
    r1 = x - hi.astype(F32)
    mid = r1.astype(BF16)
    lo = (r1 - mid.astype(F32)).astype(BF16)
    return hi, mid, lo


def _dot_exact_lhs(sel, x):
    hi, mid, lo = _split3(x)
    return _dot(sel, hi) + _dot(sel, mid) + _dot(sel, lo)


def _dot_exact_rhs(x, sel):
    hi, mid, lo = _split3(x)
    return _dot(hi, sel) + _dot(mid, sel) + _dot(lo, sel)


def _rmsnorm_kernel(x_ref, g_ref, *o_refs):
    x = x_ref[...]
    y = x * lax.rsqrt(jnp.mean(x * x, axis=-1, keepdims=True) + RMS_EPS)
    for n, o_ref in enumerate(o_refs):
        o_ref[...] = (y * g_ref[n:n + 1, :]).astype(o_ref.dtype)


def _rmsnorm(x, gains, rows=256):
    t, d = x.shape
    n = gains.shape[0]
    rows = min(rows, t)
    outs = pl.pallas_call(
        _rmsnorm_kernel,
        grid=(t // rows,),
        in_specs=[pl.BlockSpec((rows, d), lambda i: (i, 0)),
                  pl.BlockSpec((n, d), lambda i: (0, 0))],
        out_specs=[pl.BlockSpec((rows, d), lambda i: (i, 0))] * n,
        out_shape=[jax.ShapeDtypeStruct((t, d), BF16)] * n,
        compiler_params=pltpu.CompilerParams(
            dimension_semantics=("parallel",),
            vmem_limit_bytes=_vmem_limit(2 * rows * d * (4 + 2 * n))),
        name="rmsnorm",
    )(x, gains)
    return outs


def _cast_kernel(x_ref, o_ref):
    o_ref[...] = x_ref[...].astype(o_ref.dtype)


def _cast_bf16(w, layer, rows=256):
    _, k, n = w.shape
    rows = min(rows, k)
    assert k % rows == 0
    return pl.pallas_call(
        _cast_kernel,
        grid=(k // rows,),
        in_specs=[pl.BlockSpec((None, rows, n), lambda i: (layer, i, 0))],
        out_specs=pl.BlockSpec((rows, n), lambda i: (i, 0)),
        out_shape=jax.ShapeDtypeStruct((k, n), BF16),
        compiler_params=pltpu.CompilerParams(
            dimension_semantics=("parallel",),
            vmem_limit_bytes=_vmem_limit(2 * rows * n * 6)),
        name="cast_bf16",
    )(w)


def _ep_plain(accs, extra):
    return accs[0]


def _ep_resid(accs, extra):
    return accs[0] + extra[0][...]


def _ep_swiglu(accs, extra):
    g, u = accs
    return _silu(g) * u


def _ep_headnorm(accs, extra, scale):
    acc = accs[0]
    gain = extra[0][...] * scale
    outs = []
    for c in range(acc.shape[1] // HEAD_DIM):
        a = acc[:, c * HEAD_DIM:(c + 1) * HEAD_DIM]
        outs.append(a * lax.rsqrt(jnp.mean(a * a, axis=-1, keepdims=True) + RMS_EPS) * gain)
    return jnp.concatenate(outs, axis=1)


def _mm_kernel(*refs, n_w, n_extra, cast_w, inner_axis, epilogue):
    x_ref = refs[0]
    w_refs = refs[1:1 + n_w]
    extra = refs[1 + n_w:1 + n_w + n_extra]
    o_ref = refs[1 + n_w + n_extra]
    if cast_w:
        wb_refs = refs[2 + n_w + n_extra:]

        @pl.when(pl.program_id(inner_axis) == 0)
        def _():
            for w_ref, wb_ref in zip(w_refs, wb_refs):
                wb_ref[...] = w_ref[...].astype(BF16)
        w_refs = wb_refs
    accs = [_dot(x_ref[...], w_ref[...]) for w_ref in w_refs]
    o_ref[...] = epilogue(accs, extra).astype(o_ref.dtype)


def _matmul(x, w, *, n_out, bm, bn, out_dtype, epilogue=_ep_plain, w_layer=None,
            w_col_offsets=(0,), extras=(), extra_kinds=(), x_resident=False, name="matmul"):
    m, k = x.shape
    bm = min(bm, m)
    bn = min(bn, n_out)
    assert m % bm == 0 and n_out % bn == 0
    ni, nj = m // bm, n_out // bn
    cast_w = w.dtype != BF16
    assert not (cast_w and x_resident)
    if x_resident:
        grid = (ni, nj)
        ij = lambda a, b: (a, b)
        sem = ("parallel", "arbitrary")
    else:
        grid = (nj, ni)
        ij = lambda a, b: (b, a)
        sem = ("parallel", "arbitrary")

    in_specs = [pl.BlockSpec((bm, k), lambda a, b: (ij(a, b)[0], 0))]
    operands = [x]
    for off in w_col_offsets:
        assert off % bn == 0
        ob = off // bn
        if w_layer is None:
            in_specs.append(pl.BlockSpec((k, bn), lambda a, b, ob=ob: (0, ij(a, b)[1] + ob)))
        else:
            in_specs.append(pl.BlockSpec(
                (None, k, bn), lambda a, b, ob=ob: (w_layer, 0, ij(a, b)[1] + ob)))
        operands.append(w)
    extra_bytes = 0
    for e, kind in zip(extras, extra_kinds):
        if kind == "tile":
            in_specs.append(pl.BlockSpec((bm, bn), lambda a, b: ij(a, b)))
            extra_bytes += 2 * bm * bn * e.dtype.itemsize
        else:
            in_specs.append(pl.BlockSpec(e.shape, lambda a, b: (0, 0)))
        operands.append(e)
    n_w = len(w_col_offsets)
    scratch = [pltpu.VMEM((k, bn), BF16) for _ in range(n_w)] if cast_w else []
    blk = (2 * bm * k * 2 + n_w * 2 * k * bn * w.dtype.itemsize
           + (n_w * k * bn * 2 if cast_w else 0)
           + 2 * bm * bn * jnp.dtype(out_dtype).itemsize + extra_bytes)
    kern = functools.partial(_mm_kernel, n_w=n_w, n_extra=len(extras), cast_w=cast_w,
                             inner_axis=1, epilogue=epilogue)
    return pl.pallas_call(
        kern,
        grid=grid,
        in_specs=in_specs,
        out_specs=pl.BlockSpec((bm, bn), lambda a, b: ij(a, b)),
        out_shape=jax.ShapeDtypeStruct((m, n_out), out_dtype),
        scratch_shapes=scratch,
        compiler_params=pltpu.CompilerParams(
            dimension_semantics=sem, vmem_limit_bytes=_vmem_limit(blk)),
        name=name,
    )(*operands)


def _gdn_kernel(q_ref, k_ref, v_ref, z_ref, ba_ref, cq_ref, ck_ref, cv_ref, alog_ref, dtb_ref,
                e_ref, og_ref, o_ref, state_ref, pq_ref, pk_ref, pv_ref, *, hb):
    n = GDN_BLOCK
    c = pl.program_id(2)

    @pl.when(c == 0)
    def _():
        state_ref[...] = jnp.zeros_like(state_ref)
        for p_ref in (pq_ref, pk_ref, pv_ref):
            p_ref[0:SUBLANES, :] = jnp.zeros((SUBLANES, p_ref.shape[1]), F32)

    def conv_silu(x_ref, p_ref, w_ref):
        x = x_ref[...]
        p_ref[SUBLANES:SUBLANES + n, :] = x
        y = x * w_ref[GDN_CONV - 1:GDN_CONV, :]
        for j in range(GDN_CONV - 1):
            sh = GDN_CONV - 1 - j
            y = y + p_ref[SUBLANES - sh:SUBLANES - sh + n, :] * w_ref[j:j + 1, :]
        p_ref[0:SUBLANES, :] = x[n - SUBLANES:n, :]
        return _silu(y)

    qc = conv_silu(q_ref, pq_ref, cq_ref)
    kc = conv_silu(k_ref, pk_ref, ck_ref)
    vc = conv_silu(v_ref, pv_ref, cv_ref)

    row = lax.broadcasted_iota(jnp.int32, (n, n), 0)
    col = lax.broadcasted_iota(jnp.int32, (n, n), 1)
    tril = row >= col
    strict = row > col
    eye = row == col
    eye_f = jnp.where(eye, 1.0, 0.0).astype(F32)
    ltri = jnp.where(tril, 1.0, 0.0).astype(BF16)

    ba = ba_ref[...]
    hp = ba.shape[1] // 2
    beta = _sigmoid(ba[:, :hp])
    g = -jnp.exp(alog_ref[...]) * _softplus(ba[:, hp:] + dtb_ref[...])
    gcum = _dot_exact_lhs(ltri, g)
    expand = e_ref[...]
    gc_all = _dot_exact_rhs(gcum, expand)
    beta_all = _dot_exact_rhs(beta, expand)

    og = og_ref[...]
    for hh in range(hb):
        sl = slice(hh * HEAD_DIM, (hh + 1) * HEAD_DIM)
        q = qc[:, sl]
        k = kc[:, sl]
        v = vc[:, sl]
        q = q * (lax.rsqrt(jnp.sum(q * q, axis=-1, keepdims=True) + L2_EPS) * HEAD_DIM ** -0.5)
        k = k * lax.rsqrt(jnp.sum(k * k, axis=-1, keepdims=True) + L2_EPS)
        gc = gc_all[:, sl]
        bt = beta_all[:, sl]
        eg = jnp.exp(gc)
        glast = gc[n - 1:n, :]
        kb = k * bt
        grow = jnp.sum(jnp.where(eye, gc, 0.0), axis=0, keepdims=True)
        dec = jnp.where(tril, jnp.exp(jnp.where(tril, gc - grow, 0.0)), 0.0)
        kq = _dot_nt(jnp.concatenate([kb, q], axis=0).astype(BF16), k.astype(BF16))
        a_low = jnp.where(strict, kq[:n] * dec, 0.0)
        intra = kq[n:] * dec
        bm_ = -a_low
        t_inv = eye_f + bm_
        for _ in range(int(math.log2(n)) - 1):
            bb = bm_.astype(BF16)
            bm_ = _dot(bb, bb)
            t_inv = t_inv + _dot(t_inv.astype(BF16), bm_.astype(BF16))
        rhs = jnp.concatenate([v * bt, kb * eg], axis=1).astype(BF16)
        uw = _dot(t_inv.astype(BF16), rhs)
        u = uw[:, :HEAD_DIM]
        w = uw[:, HEAD_DIM:]
        s = state_ref[hh]
        ws = _dot(jnp.concatenate([w, q * eg], axis=0).astype(BF16), s.astype(BF16))
        v_new = u - ws[:n]
        v_new_b = v_new.astype(BF16)
        o = ws[n:] + _dot(intra.astype(BF16), v_new_b)
        kd = k * jnp.exp(glast - gc)
        state_ref[hh] = s * jnp.exp(glast) + _dot_tn(kd.astype(BF16), v_new_b)
        o = o * lax.rsqrt(jnp.mean(o * o, axis=-1, keepdims=True) + RMS_EPS) * og
        o_ref[:, sl] = (o * _silu(z_ref[:, sl])).astype(o_ref.dtype)


def _gdn(proj, ba, conv_w, a_log_p, dt_bias_p, expand, out_gain, *, batch, heads, hb):
    t = proj.shape[0]
    seq = t // batch
    n = GDN_BLOCK
    assert seq % n == 0 and heads % hb == 0
    nc = seq // n
    ng = heads // hb
    wd = hb * HEAD_DIM
    hp = ba.shape[1] // 2

    def tile(part):
        return pl.BlockSpec((n, wd), lambda b, g, c, part=part: (b * nc + c, part * ng + g))

    def cw(part):
        return pl.BlockSpec((GDN_CONV, wd), lambda b, g, c, part=part: (0, part * ng + g))

    whole = lambda shape: pl.BlockSpec(shape, lambda b, g, c: (0, 0))
    blk = 2 * (4 * n * wd * 4 + n * 2 * hp * 4 + n * wd * 2) + hb * HEAD_DIM * HEAD_DIM * 4 \
        + 3 * (n + SUBLANES) * wd * 4
    return pl.pallas_call(
        functools.partial(_gdn_kernel, hb=hb),
        grid=(batch, ng, nc),
        in_specs=[tile(0), tile(1), tile(2), tile(3),
                  pl.BlockSpec((n, 2 * hp), lambda b, g, c: (b * nc + c, 0)),
                  cw(0), cw(1), cw(2),
                  whole((1, hp)), whole((1, hp)),
                  pl.BlockSpec((hp, wd), lambda b, g, c: (0, g)),
                  whole((1, HEAD_DIM))],
        out_specs=pl.BlockSpec((n, wd), lambda b, g, c: (b * nc + c, g)),
        out_shape=jax.ShapeDtypeStruct((t, heads * HEAD_DIM), BF16),
        scratch_shapes=[pltpu.VMEM((hb, HEAD_DIM, HEAD_DIM), F32)]
        + [pltpu.VMEM((n + SUBLANES, wd), F32)] * 3,
        compiler_params=pltpu.CompilerParams(
            dimension_semantics=("parallel", "parallel", "arbitrary"),
            vmem_limit_bytes=_vmem_limit(blk)),
        name="gdn",
    )(proj, proj, proj, proj, ba, conv_w, conv_w, conv_w, a_log_p, dt_bias_p, expand, out_gain)


def _attn_kernel(lam_ref, q_ref, k_ref, v_ref, sg_ref, o_ref, m_ref, l_ref, acc_ref, *,
                 bq, bk, lam_init):
    qi = pl.program_id(2)
    ki = pl.program_id(3)
    d = HEAD_DIM
    last_k = ((qi + 1) * bq - 1) // bk

    @pl.when(ki == 0)
    def _():
        m_ref[...] = jnp.full_like(m_ref, -jnp.inf)
        l_ref[...] = jnp.zeros_like(l_ref)
        acc_ref[...] = jnp.zeros_like(acc_ref)

    def step(masked):
        q = q_ref[...]
        k = k_ref[...]
        v = v_ref[...]
        if masked:
            qc = (qi * bq + lax.broadcasted_iota(jnp.int32, (bq, bk), 0)) // ATTN_CHUNK
            kc = (ki * bk + lax.broadcasted_iota(jnp.int32, (bq, bk), 1)) // ATTN_CHUNK
            keep = kc <= qc
        ps = []
        for mi in range(2):
            s = _dot_nt(q[:, mi * d:(mi + 1) * d], k[:, mi * d:(mi + 1) * d])
            if masked:
                s = jnp.where(keep, s, -jnp.inf)
            m_old = m_ref[mi][:, :1]
            m_new = jnp.maximum(m_old, jnp.max(s, axis=-1, keepdims=True))
            p = jnp.exp(s - m_new)
            alpha = jnp.exp(m_old - m_new)
            l_ref[mi] = jnp.broadcast_to(
                alpha * l_ref[mi][:, :1] + jnp.sum(p, axis=-1, keepdims=True), (bq, d))
            m_ref[mi] = jnp.broadcast_to(m_new, (bq, d))
            acc_ref[mi] = acc_ref[mi] * alpha
            ps.append(p.astype(BF16))
        pv = _dot(jnp.concatenate(ps, axis=0), v)
        acc_ref[0] = acc_ref[0] + pv[:bq]
        acc_ref[1] = acc_ref[1] + pv[bq:]

    needs_mask = (ki + 1) * bk > (qi * bq // ATTN_CHUNK + 1) * ATTN_CHUNK

    @pl.when(jnp.logical_and(ki <= last_k, needs_mask))
    def _():
        step(True)

    @pl.when(jnp.logical_and(ki <= last_k, jnp.logical_not(needs_mask)))
    def _():
        step(False)

    @pl.when(ki == last_k)
    def _():
        lp = lam_ref[...]
        lam = (jnp.exp(jnp.sum(lp[0:1] * lp[1:2], axis=-1, keepdims=True))
               - jnp.exp(jnp.sum(lp[2:3] * lp[3:4], axis=-1, keepdims=True)) + lam_init)
        o = acc_ref[0] / l_ref[0][:, :1] - lam * (acc_ref[1] / l_ref[1][:, :1])
        o = o * lax.rsqrt(jnp.mean(o * o, axis=-1, keepdims=True) + RMS_EPS)
        o_ref[...] = (o * sg_ref[...] * (1.0 - lam_init)).astype(o_ref.dtype)


def _diff_attention(q, k, v, lam_params, sub_gain, *, batch, heads, lam_init, bq=512, bk=512):
    t = q.shape[0]
    seq = t // batch
    bq = min(bq, seq)
    bk = min(bk, seq)
    assert seq % bq == 0 and seq % bk == 0 and bq % ATTN_CHUNK == 0 and bk % ATTN_CHUNK == 0
    nq, nk = seq // bq, seq // bk
    hw = 2 * HEAD_DIM

    def kv_map(b, h, qi, ki):
        return (b * nk + jnp.minimum(ki, ((qi + 1) * bq - 1) // bk), h)

    blk = 2 * (bq * hw * 2 + 2 * bk * hw * 2 + bq * hw * 2) + 4 * bq * HEAD_DIM * 4 + 2 * bq * hw * 4
    return pl.pallas_call(
        functools.partial(_attn_kernel, bq=bq, bk=bk, lam_init=lam_init),
        grid=(batch, heads, nq, nk),
        in_specs=[pl.BlockSpec((4, HEAD_DIM), lambda b, h, qi, ki: (0, 0)),
                  pl.BlockSpec((bq, hw), lambda b, h, qi, ki: (b * nq + qi, h)),
                  pl.BlockSpec((bk, hw), kv_map),
                  pl.BlockSpec((bk, hw), kv_map),
                  pl.BlockSpec((1, hw), lambda b, h, qi, ki: (0, 0))],
        out_specs=pl.BlockSpec((bq, hw), lambda b, h, qi, ki: (b * nq + qi, h)),
        out_shape=jax.ShapeDtypeStruct((t, heads * hw), BF16),
        scratch_shapes=[pltpu.VMEM((2, bq, HEAD_DIM), F32),
                        pltpu.VMEM((2, bq, HEAD_DIM), F32),
                        pltpu.VMEM((2, bq, hw), F32)],
        compiler_params=pltpu.CompilerParams(
            dimension_semantics=("parallel", "parallel", "parallel", "arbitrary"),
            vmem_limit_bytes=_vmem_limit(blk)),
        name="diff_attn",
    )(lam_params, q, k, v, sub_gain)


BM = 1024
BN = 512
BN_FFN = 256


def _pad_lanes(a, width):
    return jnp.pad(a, ((0, 0), (0, width - a.shape[1])))


def _gdn_layer(x2, i, batch, a_norm, a_w_in, a_conv, a_A_log, a_dt_bias, a_out_norm, a_w_out):
    d = x2.shape[1]
    heads = a_A_log.shape[1]
    hd = heads * HEAD_DIM
    hp = HEAD_DIM
    hb = min(8, heads)
    (hn,) = _rmsnorm(x2, a_norm[i:i + 1])
    proj = _matmul(hn, a_w_in, w_layer=i, n_out=4 * hd, bm=BM, bn=BN, out_dtype=F32, name="gdn_in")
    w_ba = jnp.concatenate([_pad_lanes(a_w_in[i, :, 4 * hd:4 * hd + heads], hp),
                            _pad_lanes(a_w_in[i, :, 4 * hd + heads:], hp)], axis=1)
    ba = _matmul(hn, w_ba, n_out=2 * hp, bm=BM, bn=2 * hp, out_dtype=F32, name="gdn_in_gates")
    expand = (jnp.arange(hp)[:, None] == jnp.arange(hd)[None, :] // HEAD_DIM).astype(BF16)
    og = _gdn(proj, ba, a_conv[i], _pad_lanes(a_A_log[i:i + 1], hp), _pad_lanes(a_dt_bias[i:i + 1], hp),
              expand, a_out_norm[i:i + 1], batch=batch, heads=heads, hb=hb)
    return _matmul(og, a_w_out, w_layer=i, n_out=d, bm=BM, bn=BN, out_dtype=F32,
                   epilogue=_ep_resid, extras=(x2,), extra_kinds=("tile",), name="gdn_out")


def _ffn_layer(x2, layer, ffn_norm, ffn_w_gate_up, ffn_w_down):
    d = x2.shape[1]
    hidden = ffn_w_down.shape[1]
    (hn,) = _rmsnorm(x2, ffn_norm[layer:layer + 1])
    h = _matmul(hn, ffn_w_gate_up, w_layer=layer, n_out=hidden, bm=BM, bn=BN_FFN, out_dtype=BF16,
                epilogue=_ep_swiglu, w_col_offsets=(0, hidden), name="ffn_gate_up")
    wd = _cast_bf16(ffn_w_down, layer)
    return _matmul(h, wd, n_out=d, bm=512, bn=512, out_dtype=F32, epilogue=_ep_resid,
                   extras=(x2,), extra_kinds=("tile",), x_resident=True, name="ffn_down")


def kernel(x, a_norm, a_w_in, a_conv, a_A_log, a_dt_bias, a_out_norm, a_w_out, kv_norm, w_kv, k_norm,
           b_norm, b_w_q, b_q_norm, b_lambda, b_sub_norm, b_w_out, ffn_norm, ffn_w_gate_up, ffn_w_down):
    batch, seq, d = x.shape
    n_a = a_norm.shape[0]
    n_b = b_norm.shape[0]
    depth = n_a + n_b
    x2 = x.reshape(batch * seq, d)
    qk_width = b_w_q.shape[2]
    diff_heads = qk_width // (2 * HEAD_DIM)
    k_sh = v_sh = None
    for layer in range(depth):
        if layer < n_a:
            x2 = _gdn_layer(x2, layer, batch, a_norm, a_w_in, a_conv, a_A_log, a_dt_bias,
                            a_out_norm, a_w_out)
        else:
            j = layer - n_a
            if layer == n_a:
                kvn, hn = _rmsnorm(x2, jnp.stack([kv_norm, b_norm[j]]))
                k_sh = _matmul(kvn, w_kv, n_out=qk_width, bm=BM, bn=BN, out_dtype=BF16,
                               epilogue=functools.partial(_ep_headnorm, scale=1.0),
                               extras=(k_norm[None, :],), extra_kinds=("row",), name="kv_k")
                v_sh = _matmul(kvn, w_kv, n_out=w_kv.shape[1] - qk_width, bm=BM, bn=BN,
                               out_dtype=BF16, w_col_offsets=(qk_width,), name="kv_v")
            else:
                (hn,) = _rmsnorm(x2, b_norm[j:j + 1])
            lam_init = 0.8 - 0.6 * math.exp(-0.3 * layer)
            q = _matmul(hn, b_w_q, w_layer=j, n_out=qk_width, bm=BM, bn=BN, out_dtype=BF16,
                        epilogue=functools.partial(_ep_headnorm, scale=HEAD_DIM ** -0.5),
                        extras=(b_q_norm[j:j + 1],), extra_kinds=("row",), name="attn_q")
            ao = _diff_attention(q, k_sh, v_sh, b_lambda[j], b_sub_norm[j:j + 1],
                                 batch=batch, heads=diff_heads, lam_init=lam_init)
            x2 = _matmul(ao, b_w_out, w_layer=j, n_out=d, bm=BM, bn=BN, out_dtype=F32,
                         epilogue=_ep_resid, extras=(x2,), extra_kinds=("tile",), name="attn_out")
        x2 = _ffn_layer(x2, layer, ffn_norm, ffn_w_gate_up, ffn_w_down)
    return x2.reshape(batch, seq, d)
```

```python
import functools
import math

import jax
import jax.numpy as jnp
from jax import lax
from jax.experimental import pallas as pl
from jax.experimental.pallas import tpu as pltpu

F32 = jnp.float32
BF16 = jnp.bfloat16

RMS_EPS = 1e-6
L2_EPS = 1e-6
HEAD_DIM = 128
ATTN_CHUNK = 64
GDN_BLOCK = 128
GDN_CONV = 4
SUBLANES = 8
V7X_VMEM_CAP = 60000 * 1024


def _vmem_limit(block_bytes):
    return int(min(V7X_VMEM_CAP, block_bytes + (20 << 20)))


def _sigmoid(x):
    return 1.0 / (1.0 + jnp.exp(-x))


def _silu(x):
    return x * _sigmoid(x)


def _softplus(x):
    return jnp.maximum(x, 0.0) + jnp.log(1.0 + jnp.exp(-jnp.abs(x)))


def _dot(a, b):
    return jnp.dot(a, b, preferred_element_type=F32)


def _dot_nt(a, b):
    return lax.dot_general(a, b, (((1,), (1,)), ((), ())), preferred_element_type=F32)


def _dot_tn(a, b):
    return lax.dot_general(a, b, (((0,), (0,)), ((), ())), preferred_element_type=F32)


def _split3(x):
    hi = x.astype(BF16)
    r1 = x - hi.astype(F32)
    mid = r1.astype(BF16)
    lo = (r1 - mid.astype(F32)).astype(BF16)
    return hi, mid, lo


def _dot_exact_lhs(sel, x):
    hi, mid, lo = _split3(x)
    return _dot(sel, hi) + _dot(sel, mid) + _dot(sel, lo)


def _dot_exact_rhs(x, sel):
    hi, mid, lo = _split3(x)
    return _dot(hi, sel) + _dot(mid, sel) + _dot(lo, sel)


def _rmsnorm_kernel(x_ref, g_ref, *o_refs):
    x = x_ref[...]
    y = x * lax.rsqrt(jnp.mean(x * x, axis=-1, keepdims=True) + RMS_EPS)
    for n, o_ref in enumerate(o_refs):
        o_ref[...] = (y * g_ref[n:n + 1, :]).astype(o_ref.dtype)


def _rmsnorm(x, gains, rows=256):
    t, d = x.shape
    n = gains.shape[0]
    rows = min(rows, t)
    outs = pl.pallas_call(
        _rmsnorm_kernel,
        grid=(t // rows,),
        in_specs=[pl.BlockSpec((rows, d), lambda i: (i, 0)),
                  pl.BlockSpec((n, d), lambda i: (0, 0))],
        out_specs=[pl.BlockSpec((rows, d), lambda i: (i, 0))] * n,
        out_shape=[jax.ShapeDtypeStruct((t, d), BF16)] * n,
        compiler_params=pltpu.CompilerParams(
            dimension_semantics=("parallel",),
            vmem_limit_bytes=_vmem_limit(2 * rows * d * (4 + 2 * n))),
        name="rmsnorm",
    )(x, gains)
    return outs


def _cast_kernel(x_ref, o_ref):
    o_ref[...] = x_ref[...].astype(o_ref.dtype)


def _cast_bf16(w, layer, rows=256):
    _, k, n = w.shape
    rows = min(rows, k)
    assert k % rows == 0
    return pl.pallas_call(
        _cast_kernel,
        grid=(k // rows,),
        in_specs=[pl.BlockSpec((None, rows, n), lambda i: (layer, i, 0))],
        out_specs=pl.BlockSpec((rows, n), lambda i: (i, 0)),
        out_shape=jax.ShapeDtypeStruct((k, n), BF16),
        compiler_params=pltpu.CompilerParams(
            dimension_semantics=("parallel",),
            vmem_limit_bytes=_vmem_limit(2 * rows * n * 6)),
        name="cast_bf16",
    )(w)


def _ep_plain(accs, extra):
    return accs[0]


def _ep_resid(accs, extra):
    return accs[0] + extra[0][...]


def _ep_swiglu(accs, extra):
    g, u = accs
    return _silu(g) * u


def _ep_headnorm(accs, extra, scale):
    acc = accs[0]
    gain = extra[0][...] * scale
    outs = []
    for c in range(acc.shape[1] // HEAD_DIM):
        a = acc[:, c * HEAD_DIM:(c + 1) * HEAD_DIM]
        outs.append(a * lax.rsqrt(jnp.mean(a * a, axis=-1, keepdims=True) + RMS_EPS) * gain)
    return jnp.concatenate(outs, axis=1)


def _mm_kernel(*refs, n_w, n_extra, cast_w, inner_axis, epilogue, w_valid, w_transposed):
    x_ref = refs[0]
    w_refs = refs[1:1 + n_w]
    extra = refs[1 + n_w:1 + n_w + n_extra]
    o_ref = refs[1 + n_w + n_extra]
    if cast_w:
        wb_refs = refs[2 + n_w + n_extra:]

        @pl.when(pl.program_id(inner_axis) == 0)
        def _():
            for w_ref, wb_ref in zip(w_refs, wb_refs):
                w = w_ref[...]
                if w_valid is not None:
                    n_axis = 0 if w_transposed else 1
                    w = jnp.where(lax.broadcasted_iota(jnp.int32, w.shape, n_axis) < w_valid, w, 0.0)
                if w_transposed:
                    w = w.T
                wb_ref[...] = w.astype(BF16)
        w_refs = wb_refs
    accs = [_dot(x_ref[...], w_ref[...]) for w_ref in w_refs]
    o_ref[...] = epilogue(accs, extra).astype(o_ref.dtype)


def _matmul(x, w, *, n_out, bm, bn, out_dtype, epilogue=_ep_plain, w_layer=None,
            w_col_offsets=(0,), extras=(), extra_kinds=(), x_resident=False, w_valid=None,
            w_transposed=False, name="matmul"):
    m, k = x.shape
    bm = min(bm, m)
    bn = min(bn, n_out)
    assert m % bm == 0 and n_out % bn == 0
    ni, nj = m // bm, n_out // bn
    cast_w = w.dtype != BF16
    assert not (cast_w and x_resident)
    if x_resident:
        grid = (ni, nj)
        ij = lambda a, b: (a, b)
        sem = ("parallel", "arbitrary")
    else:
        grid = (nj, ni)
        ij = lambda a, b: (b, a)
        sem = ("parallel", "arbitrary")

    in_specs = [pl.BlockSpec((bm, k), lambda a, b: (ij(a, b)[0], 0))]
    operands = [x]
    for off in w_col_offsets:
        assert off % bn == 0
        ob = off // bn
        if w_transposed:
            in_specs.append(pl.BlockSpec(
                (None, bn, k), lambda a, b, ob=ob: (w_layer, ij(a, b)[1] + ob, 0)))
        elif w_layer is None:
            in_specs.append(pl.BlockSpec((k, bn), lambda a, b, ob=ob: (0, ij(a, b)[1] + ob)))
        else:
            in_specs.append(pl.BlockSpec(
                (None, k, bn), lambda a, b, ob=ob: (w_layer, 0, ij(a, b)[1] + ob)))
        operands.append(w)
    extra_bytes = 0
    for e, kind in zip(extras, extra_kinds):
        if kind == "tile":
            in_specs.append(pl.BlockSpec((bm, bn), lambda a, b: ij(a, b)))
            extra_bytes += 2 * bm * bn * e.dtype.itemsize
        else:
            in_specs.append(pl.BlockSpec(e.shape, lambda a, b: (0, 0)))
        operands.append(e)
    n_w = len(w_col_offsets)
    scratch = [pltpu.VMEM((k, bn), BF16) for _ in range(n_w)] if cast_w else []
    blk = (2 * bm * k * 2 + n_w * 2 * k * bn * w.dtype.itemsize
           + (n_w * k * bn * 2 if cast_w else 0)
           + 2 * bm * bn * jnp.dtype(out_dtype).itemsize + extra_bytes)
    assert (w_valid is None and not w_transposed) or cast_w
    kern = functools.partial(_mm_kernel, n_w=n_w, n_extra=len(extras), cast_w=cast_w,
                             inner_axis=1, epilogue=epilogue, w_valid=w_valid,
                             w_transposed=w_transposed)
    return pl.pallas_call(
        kern,
        grid=grid,
        in_specs=in_specs,
        out_specs=pl.BlockSpec((bm, bn), lambda a, b: ij(a, b)),
        out_shape=jax.ShapeDtypeStruct((m, n_out), out_dtype),
        scratch_shapes=scratch,
        compiler_params=pltpu.CompilerParams(
            dimension_semantics=sem, vmem_limit_bytes=_vmem_limit(blk)),
        name=name,
    )(*operands)


def _gdn_kernel(q_ref, k_ref, v_ref, z_ref, ba_ref, cq_ref, ck_ref, cv_ref, alog_ref, dtb_ref,
                eb_ref, ea_ref, og_ref, o_ref, state_ref, pq_ref, pk_ref, pv_ref, *, hb):
    n = GDN_BLOCK
    c = pl.program_id(2)

    @pl.when(c == 0)
    def _():
        state_ref[...] = jnp.zeros_like(state_ref)
        for p_ref in (pq_ref, pk_ref, pv_ref):
            p_ref[0:SUBLANES, :] = jnp.zeros((SUBLANES, p_ref.shape[1]), F32)

    def conv_silu(x_ref, p_ref, w_ref):
        x = x_ref[...]
        p_ref[SUBLANES:SUBLANES + n, :] = x
        y = x * w_ref[GDN_CONV - 1:GDN_CONV, :]
        for j in range(GDN_CONV - 1):
            sh = GDN_CONV - 1 - j
            y = y + p_ref[SUBLANES - sh:SUBLANES - sh + n, :] * w_ref[j:j + 1, :]
        p_ref[0:SUBLANES, :] = x[n - SUBLANES:n, :]
        return _silu(y)

    qc = conv_silu(q_ref, pq_ref, cq_ref)
    kc = conv_silu(k_ref, pk_ref, ck_ref)
    vc = conv_silu(v_ref, pv_ref, cv_ref)

    row = lax.broadcasted_iota(jnp.int32, (n, n), 0)
    col = lax.broadcasted_iota(jnp.int32, (n, n), 1)
    tril = row >= col
    strict = row > col
    eye = row == col
    eye_f = jnp.where(eye, 1.0, 0.0).astype(F32)
    ltri = jnp.where(tril, 1.0, 0.0).astype(BF16)

    ba = ba_ref[...]
    beta = _sigmoid(ba)
    g = -jnp.exp(alog_ref[...]) * _softplus(ba + dtb_ref[...])
    gcum = _dot_exact_lhs(ltri, g)
    gc_all = _dot_exact_rhs(gcum, ea_ref[...])
    beta_all = _dot_exact_rhs(beta, eb_ref[...])

    og = og_ref[...]
    heads = range(hb)
    sls = [slice(hh * HEAD_DIM, (hh + 1) * HEAD_DIM) for hh in heads]
    qs = [qc[:, sl] for sl in sls]
    ks = [kc[:, sl] for sl in sls]
    qs = [q * (lax.rsqrt(jnp.sum(q * q, axis=-1, keepdims=True) + L2_EPS) * HEAD_DIM ** -0.5) for q in qs]
    ks = [k * lax.rsqrt(jnp.sum(k * k, axis=-1, keepdims=True) + L2_EPS) for k in ks]
    gcs = [gc_all[:, sl] for sl in sls]
    bts = [beta_all[:, sl] for sl in sls]
    kbs = [k * bt for k, bt in zip(ks, bts)]
    grows = [jnp.sum(jnp.where(eye, gc, 0.0), axis=0, keepdims=True) for gc in gcs]
    decs = [jnp.where(tril, jnp.exp(jnp.where(tril, gc - grow, 0.0)), 0.0) for gc, grow in zip(gcs, grows)]
    kqs = [_dot_nt(jnp.concatenate([kb, q], axis=0).astype(BF16), k.astype(BF16))
           for kb, q, k in zip(kbs, qs, ks)]
    bms = [-jnp.where(strict, kq[:n] * dec, 0.0) for kq, dec in zip(kqs, decs)]
    intras = [(kq[n:] * dec).astype(BF16) for kq, dec in zip(kqs, decs)]
    tinvs = [eye_f + bm_ for bm_ in bms]
    for _ in range(int(math.log2(n)) - 1):
        bbs = [bm_.astype(BF16) for bm_ in bms]
        bms = [_dot(bb, bb) for bb in bbs]
        tinvs = [t + _dot(t.astype(BF16), bm_.astype(BF16)) for t, bm_ in zip(tinvs, bms)]
    egs = [jnp.exp(gc) for gc in gcs]
    uws = [_dot(t.astype(BF16), jnp.concatenate([vc[:, sl] * bt, kb * eg], axis=1).astype(BF16))
           for t, sl, bt, kb, eg in zip(tinvs, sls, bts, kbs, egs)]
    ss = [state_ref[hh] for hh in heads]
    wss = [_dot(jnp.concatenate([uw[:, HEAD_DIM:], q * eg], axis=0).astype(BF16), s.astype(BF16))
           for uw, q, eg, s in zip(uws, qs, egs, ss)]
    vns = [(uw[:, :HEAD_DIM] - ws[:n]).astype(BF16) for uw, ws in zip(uws, wss)]
    glasts = [gc[n - 1:n, :] for gc in gcs]
    kds = [(k * jnp.exp(glast - gc)).astype(BF16) for k, glast, gc in zip(ks, glasts, gcs)]
    for hh in heads:
        state_ref[hh] = ss[hh] * jnp.exp(glasts[hh]) + _dot_tn(kds[hh], vns[hh])
    os_ = [ws[n:] + _dot(intra, vn) for ws, intra, vn in zip(wss, intras, vns)]
    for hh in heads:
        o = os_[hh]
        o = o * lax.rsqrt(jnp.mean(o * o, axis=-1, keepdims=True) + RMS_EPS) * og
        o_ref[:, sls[hh]] = (o * _silu(z_ref[:, sls[hh]])).astype(o_ref.dtype)


def _gdn(proj, ba, conv_w, a_log_p, dt_bias_p, expand_b, expand_a, out_gain, *, batch, heads, hb):
    t = proj.shape[0]
    seq = t // batch
    n = GDN_BLOCK
    assert seq % n == 0 and heads % hb == 0
    nc = seq // n
    ng = heads // hb
    wd = hb * HEAD_DIM
    gl = ba.shape[1]

    def tile(part):
        return pl.BlockSpec((n, wd), lambda b, g, c, part=part: (b * nc + c, part * ng + g))

    def cw(part):
        return pl.BlockSpec((GDN_CONV, wd), lambda b, g, c, part=part: (0, part * ng + g))

    whole = lambda shape: pl.BlockSpec(shape, lambda b, g, c: (0, 0))
    expand_spec = pl.BlockSpec((gl, wd), lambda b, g, c: (0, g))
    blk = 2 * (4 * n * wd * 4 + n * gl * 4 + n * wd * 2 + 2 * gl * wd * 2) \
        + hb * HEAD_DIM * HEAD_DIM * 4 + 3 * (n + SUBLANES) * wd * 4
    return pl.pallas_call(
        functools.partial(_gdn_kernel, hb=hb),
        grid=(batch, ng, nc),
        in_specs=[tile(0), tile(1), tile(2), tile(3),
                  pl.BlockSpec((n, gl), lambda b, g, c: (b * nc + c, 0)),
                  cw(0), cw(1), cw(2),
                  whole((1, gl)), whole((1, gl)), expand_spec, expand_spec,
                  whole((1, HEAD_DIM))],
        out_specs=pl.BlockSpec((n, wd), lambda b, g, c: (b * nc + c, g)),
        out_shape=jax.ShapeDtypeStruct((t, heads * HEAD_DIM), BF16),
        scratch_shapes=[pltpu.VMEM((hb, HEAD_DIM, HEAD_DIM), F32)]
        + [pltpu.VMEM((n + SUBLANES, wd), F32)] * 3,
        compiler_params=pltpu.CompilerParams(
            dimension_semantics=("parallel", "parallel", "arbitrary"),
            vmem_limit_bytes=_vmem_limit(blk)),
        name="gdn",
    )(proj, proj, proj, proj, ba, conv_w, conv_w, conv_w, a_log_p, dt_bias_p, expand_b, expand_a,
      out_gain)


def _attn_kernel(lam_ref, q_ref, k_ref, v_ref, sg_ref, o_ref, m_ref, l_ref, acc_ref, *, blk, lam_init):
    qi = pl.program_id(2)
    d = HEAD_DIM
    reps = blk // d
    m_ref[...] = jnp.full_like(m_ref, -jnp.inf)
    l_ref[...] = jnp.zeros_like(l_ref)
    acc_ref[...] = jnp.zeros_like(acc_ref)

    def block(j, masked):
        start = pl.multiple_of(j * blk, blk)
        k = k_ref[pl.ds(start, blk), :]
        v = v_ref[pl.ds(start, blk), :]
        for mi in range(2):
            s = _dot_nt(q_ref[:, mi * d:(mi + 1) * d], k[:, mi * d:(mi + 1) * d])
            if masked:
                qc = lax.broadcasted_iota(jnp.int32, (blk, blk), 0) // ATTN_CHUNK
                kc = lax.broadcasted_iota(jnp.int32, (blk, blk), 1) // ATTN_CHUNK
                s = jnp.where(kc <= qc, s, -jnp.inf)
            m_old = m_ref[mi]
            m_new = jnp.maximum(m_old, jnp.max(s, axis=-1, keepdims=True))
            p = jnp.exp2(s - jnp.concatenate([m_new] * reps, axis=1))
            alpha = jnp.exp2(m_old - m_new)
            l_ref[mi] = alpha * l_ref[mi] + jnp.sum(p, axis=-1, keepdims=True)
            m_ref[mi] = m_new
            pv = _dot(p.astype(BF16), v)
            acc_ref[mi] = acc_ref[mi] * jnp.concatenate([alpha, alpha], axis=1) + pv

    def body(j, carry):
        block(j, False)
        return carry

    lax.fori_loop(0, qi, body, 0)
    block(qi, True)

    lp = lam_ref[...]
    lam = (jnp.exp(jnp.sum(lp[0:1] * lp[1:2], axis=-1, keepdims=True))
           - jnp.exp(jnp.sum(lp[2:3] * lp[3:4], axis=-1, keepdims=True)) + lam_init)
    o = (acc_ref[0] / jnp.concatenate([l_ref[0]] * 2, axis=1)
         - lam * (acc_ref[1] / jnp.concatenate([l_ref[1]] * 2, axis=1)))
    o = o * lax.rsqrt(jnp.mean(o * o, axis=-1, keepdims=True) + RMS_EPS)
    o_ref[...] = (o * sg_ref[...] * (1.0 - lam_init)).astype(o_ref.dtype)


def _diff_attention(q, k, v, lam_params, sub_gain, *, batch, heads, lam_init, blk=512):
    t = q.shape[0]
    seq = t // batch
    blk = min(blk, seq)
    assert seq % blk == 0 and blk % ATTN_CHUNK == 0
    nq = seq // blk
    hw = 2 * HEAD_DIM
    vmem = 2 * (2 * blk * hw * 2 + 2 * seq * hw * 2) + 4 * blk * HEAD_DIM * 4 + 2 * blk * hw * 4
    return pl.pallas_call(
        functools.partial(_attn_kernel, blk=blk, lam_init=lam_init),
        grid=(batch, heads, nq),
        in_specs=[pl.BlockSpec((4, HEAD_DIM), lambda b, h, qi: (0, 0)),
                  pl.BlockSpec((blk, hw), lambda b, h, qi: (b * nq + qi, h)),
                  pl.BlockSpec((seq, hw), lambda b, h, qi: (b, h)),
                  pl.BlockSpec((seq, hw), lambda b, h, qi: (b, h)),
                  pl.BlockSpec((1, hw), lambda b, h, qi: (0, 0))],
        out_specs=pl.BlockSpec((blk, hw), lambda b, h, qi: (b * nq + qi, h)),
        out_shape=jax.ShapeDtypeStruct((t, heads * hw), BF16),
        scratch_shapes=[pltpu.VMEM((2, blk, HEAD_DIM), F32),
                        pltpu.VMEM((2, blk, HEAD_DIM), F32),
                        pltpu.VMEM((2, blk, hw), F32)],
        compiler_params=pltpu.CompilerParams(
            dimension_semantics=("parallel", "parallel", "parallel"),
            vmem_limit_bytes=_vmem_limit(vmem)),
        name="diff_attn",
    )(lam_params, q, k, v, sub_gain)


BM = 1024
BN = 512
BN_FFN = 256


def _pad_lanes(a, width):
    return jnp.pad(a, ((0, 0), (0, width - a.shape[1])))


def _gdn_layer(x2, i, batch, a_norm, a_w_in, a_conv, a_A_log, a_dt_bias, a_out_norm, a_w_out):
    d = x2.shape[1]
    heads = a_A_log.shape[1]
    hd = heads * HEAD_DIM
    gl = HEAD_DIM
    assert 2 * heads <= gl and a_w_in.shape[2] == 4 * hd + 2 * heads
    hb = min(8, heads)
    (hn,) = _rmsnorm(x2, a_norm[i:i + 1])
    w_in_t = jnp.swapaxes(a_w_in, 1, 2)
    proj = _matmul(hn, w_in_t, w_layer=i, n_out=4 * hd, bm=BM, bn=BN, out_dtype=F32,
                   w_transposed=True, name="gdn_in")
    ba = _matmul(hn, w_in_t, w_layer=i, n_out=gl, bm=BM, bn=gl, out_dtype=F32,
                 w_col_offsets=(4 * hd,), w_valid=2 * heads, w_transposed=True, name="gdn_in_gates")
    lane = jnp.arange(gl)[:, None]
    head_of_col = jnp.arange(hd)[None, :] // HEAD_DIM
    expand_b = (lane == head_of_col).astype(BF16)
    expand_a = (lane - heads == head_of_col).astype(BF16)
    on_a_lanes = lambda p: jnp.pad(p[i:i + 1], ((0, 0), (heads, gl - 2 * heads)))
    og = _gdn(proj, ba, a_conv[i], on_a_lanes(a_A_log), on_a_lanes(a_dt_bias), expand_b, expand_a,
              a_out_norm[i:i + 1], batch=batch, heads=heads, hb=hb)
    return _matmul(og, a_w_out, w_layer=i, n_out=d, bm=BM, bn=BN, out_dtype=F32,
                   epilogue=_ep_resid, extras=(x2,), extra_kinds=("tile",), name="gdn_out")


def _ffn_layer(x2, layer, ffn_norm, ffn_w_gate_up, ffn_w_down):
    d = x2.shape[1]
    hidden = ffn_w_down.shape[1]
    (hn,) = _rmsnorm(x2, ffn_norm[layer:layer + 1])
    h = _matmul(hn, ffn_w_gate_up, w_layer=layer, n_out=hidden, bm=BM, bn=BN_FFN, out_dtype=BF16,
                epilogue=_ep_swiglu, w_col_offsets=(0, hidden), name="ffn_gate_up")
    wd = _cast_bf16(ffn_w_down, layer)
    return _matmul(h, wd, n_out=d, bm=512, bn=512, out_dtype=F32, epilogue=_ep_resid,
                   extras=(x2,), extra_kinds=("tile",), x_resident=True, name="ffn_down")


def kernel(x, a_norm, a_w_in, a_conv, a_A_log, a_dt_bias, a_out_norm, a_w_out, kv_norm, w_kv, k_norm,
           b_norm, b_w_q, b_q_norm, b_lambda, b_sub_norm, b_w_out, ffn_norm, ffn_w_gate_up, ffn_w_down):
    batch, seq, d = x.shape
    n_a = a_norm.shape[0]
    n_b = b_norm.shape[0]
    depth = n_a + n_b
    x2 = x.reshape(batch * seq, d)
    qk_width = b_w_q.shape[2]
    diff_heads = qk_width // (2 * HEAD_DIM)
    k_sh = v_sh = None
    for layer in range(depth):
        if layer < n_a:
            x2 = _gdn_layer(x2, layer, batch, a_norm, a_w_in, a_conv, a_A_log, a_dt_bias,
                            a_out_norm, a_w_out)
        else:
            j = layer - n_a
            if layer == n_a:
                kvn, hn = _rmsnorm(x2, jnp.stack([kv_norm, b_norm[j]]))
                k_sh = _matmul(kvn, w_kv, n_out=qk_width, bm=BM, bn=BN, out_dtype=BF16,
                               epilogue=functools.partial(_ep_headnorm, scale=1.0),
                               extras=(k_norm[None, :],), extra_kinds=("row",), name="kv_k")
                v_sh = _matmul(kvn, w_kv, n_out=w_kv.shape[1] - qk_width, bm=BM, bn=BN,
                               out_dtype=BF16, w_col_offsets=(qk_width,), name="kv_v")
            else:
                (hn,) = _rmsnorm(x2, b_norm[j:j + 1])
            lam_init = 0.8 - 0.6 * math.exp(-0.3 * layer)
            q = _matmul(hn, b_w_q, w_layer=j, n_out=qk_width, bm=BM, bn=BN, out_dtype=BF16,
                        epilogue=functools.partial(_ep_headnorm, scale=HEAD_DIM ** -0.5 * math.log2(math.e)),
                        extras=(b_q_norm[j:j + 1],), extra_kinds=("row",), name="attn_q")
            ao = _diff_attention(q, k_sh, v_sh, b_lambda[j], b_sub_norm[j:j + 1],
                                 batch=batch, heads=diff_heads, lam_init=lam_init)
            x2 = _matmul(ao, b_w_out, w_layer=j, n_out=d, bm=BM, bn=BN, out_dtype=F32,
                         epilogue=_ep_resid, extras=(x2,), extra_kinds=("tile",), name="attn_out")
        x2 = _ffn_layer(x2, layer, ffn_norm, ffn_w_gate_up, ffn_w_down)
    return x2.reshape(batch, seq, d)
```

```python
import functools
import math

import jax
import jax.numpy as jnp
from jax import lax
from jax.experimental import pallas as pl
from jax.experimental.pallas import tpu as pltpu

F32 = jnp.float32
BF16 = jnp.bfloat16

RMS_EPS = 1e-6
L2_EPS = 1e-6
HEAD_DIM = 128
ATTN_CHUNK = 64
GDN_BLOCK = 128
GDN_CONV = 4
SUBLANES = 8
V7X_VMEM_CAP = 60000 * 1024


def _vmem_limit(block_bytes):
    return int(min(V7X_VMEM_CAP, block_bytes + (20 << 20)))


def _sigmoid(x):
    return 1.0 / (1.0 + jnp.exp(-x))


def _silu(x):
    return x * _sigmoid(x)


def _softplus(x):
    return jnp.maximum(x, 0.0) + jnp.log(1.0 + jnp.exp(-jnp.abs(x)))


def _dot(a, b):
    return jnp.dot(a, b, preferred_element_type=F32)


def _dot_nt(a, b):
    return lax.dot_general(a, b, (((1,), (1,)), ((), ())), preferred_element_type=F32)


def _dot_tn(a, b):
    return lax.dot_general(a, b, (((0,), (0,)), ((), ())), preferred_element_type=F32)


def _split3(x):
    hi = x.astype(BF16)
    r1 = x - hi.astype(F32)
    mid = r1.astype(BF16)
    lo = (r1 - mid.astype(F32)).astype(BF16)
    return hi, mid, lo


def _dot_exact_lhs(sel, x):
    hi, mid, lo = _split3(x)
    return _dot(sel, hi) + _dot(sel, mid) + _dot(sel, lo)


def _rmsnorm_kernel(x_ref, g_ref, *o_refs):
    x = x_ref[...]
    y = x * lax.rsqrt(jnp.mean(x * x, axis=-1, keepdims=True) + RMS_EPS)
    for n, o_ref in enumerate(o_refs):
        o_ref[...] = (y * g_ref[n:n + 1, :]).astype(o_ref.dtype)


def _rmsnorm(x, gains, rows=256):
    t, d = x.shape
    n = gains.shape[0]
    rows = min(rows, t)
    outs = pl.pallas_call(
        _rmsnorm_kernel,
        grid=(t // rows,),
        in_specs=[pl.BlockSpec((rows, d), lambda i: (i, 0)),
                  pl.BlockSpec((n, d), lambda i: (0, 0))],
        out_specs=[pl.BlockSpec((rows, d), lambda i: (i, 0))] * n,
        out_shape=[jax.ShapeDtypeStruct((t, d), BF16)] * n,
        compiler_params=pltpu.CompilerParams(
            dimension_semantics=("parallel",),
            vmem_limit_bytes=_vmem_limit(2 * rows * d * (4 + 2 * n))),
        name="rmsnorm",
    )(x, gains)
    return outs


def _cast_kernel(x_ref, o_ref):
    o_ref[...] = x_ref[...].astype(o_ref.dtype)


def _cast_bf16(w, layer, rows=256):
    _, k, n = w.shape
    rows = min(rows, k)
    assert k % rows == 0
    return pl.pallas_call(
        _cast_kernel,
        grid=(k // rows,),
        in_specs=[pl.BlockSpec((None, rows, n), lambda i: (layer, i, 0))],
        out_specs=pl.BlockSpec((rows, n), lambda i: (i, 0)),
        out_shape=jax.ShapeDtypeStruct((k, n), BF16),
        compiler_params=pltpu.CompilerParams(
            dimension_semantics=("parallel",),
            vmem_limit_bytes=_vmem_limit(2 * rows * n * 6)),
        name="cast_bf16",
    )(w)


def _ep_plain(accs, extra):
    return accs[0]


def _ep_resid(accs, extra):
    return accs[0] + extra[0][...]


def _ep_swiglu(accs, extra):
    g, u = accs
    return _silu(g) * u


def _ep_headnorm(accs, extra, scale):
    acc = accs[0]
    gain = extra[0][...] * scale
    outs = []
    for c in range(acc.shape[1] // HEAD_DIM):
        a = acc[:, c * HEAD_DIM:(c + 1) * HEAD_DIM]
        outs.append(a * lax.rsqrt(jnp.mean(a * a, axis=-1, keepdims=True) + RMS_EPS) * gain)
    return jnp.concatenate(outs, axis=1)


def _mm_kernel(*refs, n_w, n_extra, cast_w, inner_axis, epilogue, w_valid, w_transposed):
    x_ref = refs[0]
    w_refs = refs[1:1 + n_w]
    extra = refs[1 + n_w:1 + n_w + n_extra]
    o_ref = refs[1 + n_w + n_extra]
    if cast_w:
        wb_refs = refs[2 + n_w + n_extra:]

        @pl.when(pl.program_id(inner_axis) == 0)
        def _():
            for w_ref, wb_ref in zip(w_refs, wb_refs):
                w = w_ref[...]
                if w_valid is not None:
                    n_axis = 0 if w_transposed else 1
                    w = jnp.where(lax.broadcasted_iota(jnp.int32, w.shape, n_axis) < w_valid, w, 0.0)
                if w_transposed:
                    w = w.T
                wb_ref[...] = w.astype(BF16)
        w_refs = wb_refs
    accs = [_dot(x_ref[...], w_ref[...]) for w_ref in w_refs]
    o_ref[...] = epilogue(accs, extra).astype(o_ref.dtype)


def _matmul(x, w, *, n_out, bm, bn, out_dtype, epilogue=_ep_plain, w_layer=None,
            w_col_offsets=(0,), extras=(), extra_kinds=(), x_resident=False, w_valid=None,
            w_transposed=False, name="matmul"):
    m, k = x.shape
    bm = min(bm, m)
    bn = min(bn, n_out)
    assert m % bm == 0 and n_out % bn == 0
    ni, nj = m // bm, n_out // bn
    cast_w = w.dtype != BF16
    assert not (cast_w and x_resident)
    if x_resident:
        grid = (ni, nj)
        ij = lambda a, b: (a, b)
        sem = ("parallel", "arbitrary")
    else:
        grid = (nj, ni)
        ij = lambda a, b: (b, a)
        sem = ("parallel", "arbitrary")

    in_specs = [pl.BlockSpec((bm, k), lambda a, b: (ij(a, b)[0], 0))]
    operands = [x]
    for off in w_col_offsets:
        assert off % bn == 0
        ob = off // bn
        if w_transposed:
            in_specs.append(pl.BlockSpec(
                (None, bn, k), lambda a, b, ob=ob: (w_layer, ij(a, b)[1] + ob, 0)))
        elif w_layer is None:
            in_specs.append(pl.BlockSpec((k, bn), lambda a, b, ob=ob: (0, ij(a, b)[1] + ob)))
        else:
            in_specs.append(pl.BlockSpec(
                (None, k, bn), lambda a, b, ob=ob: (w_layer, 0, ij(a, b)[1] + ob)))
        operands.append(w)
    extra_bytes = 0
    for e, kind in zip(extras, extra_kinds):
        if kind == "tile":
            in_specs.append(pl.BlockSpec((bm, bn), lambda a, b: ij(a, b)))
            extra_bytes += 2 * bm * bn * e.dtype.itemsize
        else:
            in_specs.append(pl.BlockSpec(e.shape, lambda a, b: (0, 0)))
        operands.append(e)
    n_w = len(w_col_offsets)
    scratch = [pltpu.VMEM((k, bn), BF16) for _ in range(n_w)] if cast_w else []
    blk = (2 * bm * k * 2 + n_w * 2 * k * bn * w.dtype.itemsize
           + (n_w * k * bn * 2 if cast_w else 0)
           + 2 * bm * bn * jnp.dtype(out_dtype).itemsize + extra_bytes)
    assert (w_valid is None and not w_transposed) or cast_w
    kern = functools.partial(_mm_kernel, n_w=n_w, n_extra=len(extras), cast_w=cast_w,
                             inner_axis=1, epilogue=epilogue, w_valid=w_valid,
                             w_transposed=w_transposed)
    return pl.pallas_call(
        kern,
        grid=grid,
        in_specs=in_specs,
        out_specs=pl.BlockSpec((bm, bn), lambda a, b: ij(a, b)),
        out_shape=jax.ShapeDtypeStruct((m, n_out), out_dtype),
        scratch_shapes=scratch,
        compiler_params=pltpu.CompilerParams(
            dimension_semantics=sem, vmem_limit_bytes=_vmem_limit(blk)),
        name=name,
    )(*operands)


def _gdn_kernel(q_ref, k_ref, v_ref, z_ref, ba_ref, cq_ref, ck_ref, cv_ref, alog_ref, dtb_ref,
                og_ref, o_ref, state_ref, pq_ref, pk_ref, pv_ref, *, hb, n_heads):
    n = GDN_BLOCK
    c = pl.program_id(2)

    @pl.when(c == 0)
    def _():
        state_ref[...] = jnp.zeros_like(state_ref)
        for p_ref in (pq_ref, pk_ref, pv_ref):
            p_ref[...] = jnp.zeros_like(p_ref)

    row8 = lax.broadcasted_iota(jnp.int32, (SUBLANES, q_ref.shape[1]), 0)

    def conv_silu(x_ref, p_ref, w_ref):
        x = x_ref[...]
        prev = p_ref[...]
        y = x * w_ref[GDN_CONV - 1:GDN_CONV, :]
        for j in range(GDN_CONV - 1):
            sh = GDN_CONV - 1 - j
            rolled = pltpu.roll(x, sh, 0)
            first = jnp.where(row8 < sh, pltpu.roll(prev, sh, 0), rolled[0:SUBLANES, :])
            shifted = jnp.concatenate([first, rolled[SUBLANES:, :]], axis=0)
            y = y + shifted * w_ref[j:j + 1, :]
        p_ref[...] = x[n - SUBLANES:n, :]
        return _silu(y)

    qc = conv_silu(q_ref, pq_ref, cq_ref)
    kc = conv_silu(k_ref, pk_ref, ck_ref)
    vc = conv_silu(v_ref, pv_ref, cv_ref)

    row = lax.broadcasted_iota(jnp.int32, (n, n), 0)
    col = lax.broadcasted_iota(jnp.int32, (n, n), 1)
    tril = row >= col
    strict = row > col
    eye = row == col
    eye_f = jnp.where(eye, 1.0, 0.0).astype(F32)
    ltri = jnp.where(tril, 1.0, 0.0).astype(BF16)

    ba = ba_ref[...]
    gl = ba.shape[1]
    shift = lax.rem(gl - pl.program_id(1) * hb, gl)
    beta = pltpu.roll(_sigmoid(ba), shift, 1)
    g = -jnp.exp(alog_ref[...]) * _softplus(ba + dtb_ref[...])
    gcum = pltpu.roll(_dot_exact_lhs(ltri, g), shift, 1)
    gcum_t = gcum.T

    og = og_ref[...]
    heads = range(hb)
    sls = [slice(hh * HEAD_DIM, (hh + 1) * HEAD_DIM) for hh in heads]
    qs = [qc[:, sl] for sl in sls]
    ks = [kc[:, sl] for sl in sls]
    qs = [q * (lax.rsqrt(jnp.sum(q * q, axis=-1, keepdims=True) + L2_EPS) * HEAD_DIM ** -0.5) for q in qs]
    ks = [k * lax.rsqrt(jnp.sum(k * k, axis=-1, keepdims=True) + L2_EPS) for k in ks]
    gcs = [jnp.broadcast_to(gcum[:, n_heads + hh:n_heads + hh + 1], (n, HEAD_DIM)) for hh in heads]
    bts = [jnp.broadcast_to(beta[:, hh:hh + 1], (n, HEAD_DIM)) for hh in heads]
    grows = [gcum_t[n_heads + hh:n_heads + hh + 1, :] for hh in heads]
    kbs = [k * bt for k, bt in zip(ks, bts)]
    decs = [jnp.where(tril, jnp.exp(gc - grow), 0.0) for gc, grow in zip(gcs, grows)]
    kqs = [_dot_nt(jnp.concatenate([kb, q], axis=0).astype(BF16), k.astype(BF16))
           for kb, q, k in zip(kbs, qs, ks)]
    bms = [-jnp.where(strict, kq[:n] * dec, 0.0) for kq, dec in zip(kqs, decs)]
    intras = [(kq[n:] * dec).astype(BF16) for kq, dec in zip(kqs, decs)]
    tinvs = [eye_f + bm_ for bm_ in bms]
    for _ in range(int(math.log2(n)) - 1):
        bbs = [bm_.astype(BF16) for bm_ in bms]
        bms = [_dot(bb, bb) for bb in bbs]
        tinvs = [t + _dot(t.astype(BF16), bm_.astype(BF16)) for t, bm_ in zip(tinvs, bms)]
    egs = [jnp.exp(gc) for gc in gcs]
    uws = [_dot(t.astype(BF16), jnp.concatenate([vc[:, sl] * bt, kb * eg], axis=1).astype(BF16))
           for t, sl, bt, kb, eg in zip(tinvs, sls, bts, kbs, egs)]
    ss = [state_ref[hh] for hh in heads]
    wss = [_dot(jnp.concatenate([uw[:, HEAD_DIM:], q * eg], axis=0).astype(BF16), s.astype(BF16))
           for uw, q, eg, s in zip(uws, qs, egs, ss)]
    vns = [(uw[:, :HEAD_DIM] - ws[:n]).astype(BF16) for uw, ws in zip(uws, wss)]
    glasts = [gc[n - 1:n, :] for gc in gcs]
    kds = [(k * jnp.exp(glast - gc)).astype(BF16) for k, glast, gc in zip(ks, glasts, gcs)]
    for hh in heads:
        state_ref[hh] = ss[hh] * jnp.exp(glasts[hh]) + _dot_tn(kds[hh], vns[hh])
    os_ = [ws[n:] + _dot(intra, vn) for ws, intra, vn in zip(wss, intras, vns)]
    for hh in heads:
        o = os_[hh]
        o = o * lax.rsqrt(jnp.mean(o * o, axis=-1, keepdims=True) + RMS_EPS) * og
        o_ref[:, sls[hh]] = (o * _silu(z_ref[:, sls[hh]])).astype(o_ref.dtype)


def _gdn(proj, ba, conv_w, a_log_p, dt_bias_p, out_gain, *, batch, heads, hb):
    t = proj.shape[0]
    seq = t // batch
    n = GDN_BLOCK
    assert seq % n == 0 and heads % hb == 0
    nc = seq // n
    ng = heads // hb
    wd = hb * HEAD_DIM
    gl = ba.shape[1]

    def tile(part):
        return pl.BlockSpec((n, wd), lambda b, g, c, part=part: (b * nc + c, part * ng + g))

    def cw(part):
        return pl.BlockSpec((GDN_CONV, wd), lambda b, g, c, part=part: (0, part * ng + g))

    whole = lambda shape: pl.BlockSpec(shape, lambda b, g, c: (0, 0))
    blk = 2 * (4 * n * wd * 4 + n * gl * 4 + n * wd * 2) \
        + hb * HEAD_DIM * HEAD_DIM * 4 + 3 * SUBLANES * wd * 4
    return pl.pallas_call(
        functools.partial(_gdn_kernel, hb=hb, n_heads=heads),
        grid=(batch, ng, nc),
        in_specs=[tile(0), tile(1), tile(2), tile(3),
                  pl.BlockSpec((n, gl), lambda b, g, c: (b * nc + c, 0)),
                  cw(0), cw(1), cw(2),
                  whole((1, gl)), whole((1, gl)), whole((1, HEAD_DIM))],
        out_specs=pl.BlockSpec((n, wd), lambda b, g, c: (b * nc + c, g)),
        out_shape=jax.ShapeDtypeStruct((t, heads * HEAD_DIM), BF16),
        scratch_shapes=[pltpu.VMEM((hb, HEAD_DIM, HEAD_DIM), F32)]
        + [pltpu.VMEM((SUBLANES, wd), F32)] * 3,
        compiler_params=pltpu.CompilerParams(
            dimension_semantics=("parallel", "parallel", "arbitrary"),
            vmem_limit_bytes=_vmem_limit(blk)),
        name="gdn",
    )(proj, proj, proj, proj, ba, conv_w, conv_w, conv_w, a_log_p, dt_bias_p, out_gain)


def _attn_kernel(lam_ref, q_ref, k_ref, v_ref, sg_ref, o_ref, m_ref, l_ref, acc_ref, *, blk, row_split,
                 lam_init):
    qi = pl.program_id(2)
    d = HEAD_DIM
    m_ref[...] = jnp.full_like(m_ref, -jnp.inf)
    l_ref[...] = jnp.zeros_like(l_ref)
    acc_ref[...] = jnp.zeros_like(acc_ref)

    rows = blk // row_split

    def block(j, masked):
        start = pl.multiple_of(j * blk, blk)
        chains = [(mi, r) for r in range(row_split) for mi in range(2)]

        def nkeys(r):
            return (r + 1) * rows if masked else blk

        def rsl(r):
            return slice(r * rows, (r + 1) * rows)

        ss = [_dot_nt(q_ref[rsl(r), mi * d:(mi + 1) * d],
                      k_ref[pl.ds(start, nkeys(r)), mi * d:(mi + 1) * d]) for mi, r in chains]
        if masked:
            def mask(s, r):
                qc = (r * rows + lax.broadcasted_iota(jnp.int32, s.shape, 0)) // ATTN_CHUNK
                kc = lax.broadcasted_iota(jnp.int32, s.shape, 1) // ATTN_CHUNK
                return jnp.where(kc <= qc, s, -jnp.inf)
            ss = [mask(s, r) for s, (mi, r) in zip(ss, chains)]
        m_olds = [m_ref[mi, rsl(r), :] for mi, r in chains]
        m_news = [jnp.maximum(m_old, jnp.max(s, axis=-1, keepdims=True)) for m_old, s in zip(m_olds, ss)]
        ps = [jnp.exp2(s - jnp.concatenate([m_new] * (s.shape[1] // d), axis=1))
              for s, m_new in zip(ss, m_news)]
        alphas = [jnp.exp2(m_old - m_new) for m_old, m_new in zip(m_olds, m_news)]
        for (mi, r), alpha, p, m_new in zip(chains, alphas, ps, m_news):
            l_ref[mi, rsl(r), :] = alpha * l_ref[mi, rsl(r), :] + jnp.sum(p, axis=-1, keepdims=True)
            m_ref[mi, rsl(r), :] = m_new
        pvs = [_dot(p.astype(BF16), v_ref[pl.ds(start, nkeys(r)), :]) for p, (mi, r) in zip(ps, chains)]
        for (mi, r), alpha, pv in zip(chains, alphas, pvs):
            acc_ref[mi, rsl(r), :] = acc_ref[mi, rsl(r), :] * jnp.concatenate([alpha, alpha], axis=1) + pv

    def body(j, carry):
        block(j, False)
        return carry

    lax.fori_loop(0, qi, body, 0)
    block(qi, True)

    lp = lam_ref[...]
    lam = (jnp.exp(jnp.sum(lp[0:1] * lp[1:2], axis=-1, keepdims=True))
           - jnp.exp(jnp.sum(lp[2:3] * lp[3:4], axis=-1, keepdims=True)) + lam_init)
    o = (acc_ref[0] / jnp.concatenate([l_ref[0]] * 2, axis=1)
         - lam * (acc_ref[1] / jnp.concatenate([l_ref[1]] * 2, axis=1)))
    o = o * lax.rsqrt(jnp.mean(o * o, axis=-1, keepdims=True) + RMS_EPS)
    o_ref[...] = (o * sg_ref[...] * (1.0 - lam_init)).astype(o_ref.dtype)


def _diff_attention(q, k, v, lam_params, sub_gain, *, batch, heads, lam_init, blk=512, row_split=2):
    t = q.shape[0]
    seq = t // batch
    blk = min(blk, seq)
    assert seq % blk == 0 and blk % (row_split * ATTN_CHUNK) == 0
    nq = seq // blk
    hw = 2 * HEAD_DIM
    vmem = 2 * (2 * blk * hw * 2 + 2 * seq * hw * 2) + 4 * blk * HEAD_DIM * 4 + 2 * blk * hw * 4
    return pl.pallas_call(
        functools.partial(_attn_kernel, blk=blk, row_split=row_split, lam_init=lam_init),
        grid=(batch, heads, nq),
        in_specs=[pl.BlockSpec((4, HEAD_DIM), lambda b, h, qi: (0, 0)),
                  pl.BlockSpec((blk, hw), lambda b, h, qi: (b * nq + qi, h)),
                  pl.BlockSpec((seq, hw), lambda b, h, qi: (b, h)),
                  pl.BlockSpec((seq, hw), lambda b, h, qi: (b, h)),
                  pl.BlockSpec((1, hw), lambda b, h, qi: (0, 0))],
        out_specs=pl.BlockSpec((blk, hw), lambda b, h, qi: (b * nq + qi, h)),
        out_shape=jax.ShapeDtypeStruct((t, heads * hw), BF16),
        scratch_shapes=[pltpu.VMEM((2, blk, HEAD_DIM), F32),
                        pltpu.VMEM((2, blk, HEAD_DIM), F32),
                        pltpu.VMEM((2, blk, hw), F32)],
        compiler_params=pltpu.CompilerParams(
            dimension_semantics=("parallel", "parallel", "parallel"),
            vmem_limit_bytes=_vmem_limit(vmem)),
        name="diff_attn",
    )(lam_params, q, k, v, sub_gain)


BM = 1024
BN = 512
BN_FFN = 256
GDN_HEADS_PER_STEP = 16


def _gdn_layer(x2, i, batch, a_norm, a_w_in, a_conv, a_A_log, a_dt_bias, a_out_norm, a_w_out):
    d = x2.shape[1]
    heads = a_A_log.shape[1]
    hd = heads * HEAD_DIM
    gl = HEAD_DIM
    assert 2 * heads <= gl and a_w_in.shape[2] == 4 * hd + 2 * heads
    hb = min(GDN_HEADS_PER_STEP, heads)
    (hn,) = _rmsnorm(x2, a_norm[i:i + 1])
    w_in_t = jnp.swapaxes(a_w_in, 1, 2)
    proj = _matmul(hn, w_in_t, w_layer=i, n_out=4 * hd, bm=BM, bn=BN, out_dtype=F32,
                   w_transposed=True, name="gdn_in")
    ba = _matmul(hn, w_in_t, w_layer=i, n_out=gl, bm=BM, bn=gl, out_dtype=F32,
                 w_col_offsets=(4 * hd,), w_valid=2 * heads, w_transposed=True, name="gdn_in_gates")
    on_a_lanes = lambda p: jnp.pad(p[i:i + 1], ((0, 0), (heads, gl - 2 * heads)))
    og = _gdn(proj, ba, a_conv[i], on_a_lanes(a_A_log), on_a_lanes(a_dt_bias), a_out_norm[i:i + 1],
              batch=batch, heads=heads, hb=hb)
    return _matmul(og, a_w_out, w_layer=i, n_out=d, bm=BM, bn=BN, out_dtype=F32,
                   epilogue=_ep_resid, extras=(x2,), extra_kinds=("tile",), name="gdn_out")


def _ffn_layer(x2, layer, ffn_norm, ffn_w_gate_up, ffn_w_down):
    d = x2.shape[1]
    hidden = ffn_w_down.shape[1]
    (hn,) = _rmsnorm(x2, ffn_norm[layer:layer + 1])
    h = _matmul(hn, ffn_w_gate_up, w_layer=layer, n_out=hidden, bm=BM, bn=BN_FFN, out_dtype=BF16,
                epilogue=_ep_swiglu, w_col_offsets=(0, hidden), name="ffn_gate_up")
    wd = _cast_bf16(ffn_w_down, layer)
    return _matmul(h, wd, n_out=d, bm=512, bn=512, out_dtype=F32, epilogue=_ep_resid,
                   extras=(x2,), extra_kinds=("tile",), x_resident=True, name="ffn_down")


def kernel(x, a_norm, a_w_in, a_conv, a_A_log, a_dt_bias, a_out_norm, a_w_out, kv_norm, w_kv, k_norm,
           b_norm, b_w_q, b_q_norm, b_lambda, b_sub_norm, b_w_out, ffn_norm, ffn_w_gate_up, ffn_w_down):
    batch, seq, d = x.shape
    n_a = a_norm.shape[0]
    n_b = b_norm.shape[0]
    depth = n_a + n_b
    x2 = x.reshape(batch * seq, d)
    qk_width = b_w_q.shape[2]
    diff_heads = qk_width // (2 * HEAD_DIM)
    k_sh = v_sh = None
    for layer in range(depth):
        if layer < n_a:
            x2 = _gdn_layer(x2, layer, batch, a_norm, a_w_in, a_conv, a_A_log, a_dt_bias,
                            a_out_norm, a_w_out)
        else:
            j = layer - n_a
            if layer == n_a:
                kvn, hn = _rmsnorm(x2, jnp.stack([kv_norm, b_norm[j]]))
                k_sh = _matmul(kvn, w_kv, n_out=qk_width, bm=BM, bn=BN, out_dtype=BF16,
                               epilogue=functools.partial(_ep_headnorm, scale=1.0),
                               extras=(k_norm[None, :],), extra_kinds=("row",), name="kv_k")
                v_sh = _matmul(kvn, w_kv, n_out=w_kv.shape[1] - qk_width, bm=BM, bn=BN,
                               out_dtype=BF16, w_col_offsets=(qk_width,), name="kv_v")
            else:
                (hn,) = _rmsnorm(x2, b_norm[j:j + 1])
            lam_init = 0.8 - 0.6 * math.exp(-0.3 * layer)
            q = _matmul(hn, b_w_q, w_layer=j, n_out=qk_width, bm=BM, bn=BN, out_dtype=BF16,
                        epilogue=functools.partial(_ep_headnorm, scale=HEAD_DIM ** -0.5 * math.log2(math.e)),
                        extras=(b_q_norm[j:j + 1],), extra_kinds=("row",), name="attn_q")
            ao = _diff_attention(q, k_sh, v_sh, b_lambda[j], b_sub_norm[j:j + 1],
                                 batch=batch, heads=diff_heads, lam_init=lam_init)
            x2 = _matmul(ao, b_w_out, w_layer=j, n_out=d, bm=BM, bn=BN, out_dtype=F32,
                         epilogue=_ep_resid, extras=(x2,), extra_kinds=("tile",), name="attn_out")
        x2 = _ffn_layer(x2, layer, ffn_norm, ffn_w_gate_up, ffn_w_down)
    return x2.reshape(batch, seq, d)
```

```python
import functools
import math

import jax
import jax.numpy as jnp
from jax import lax
from jax.experimental import pallas as pl
from jax.experimental.pallas import tpu as pltpu

F32 = jnp.float32
BF16 = jnp.bfloat16

RMS_EPS = 1e-6
L2_EPS = 1e-6
HEAD_DIM = 128
ATTN_CHUNK = 64
GDN_BLOCK = 128
GDN_CONV = 4
SUBLANES = 8
V7X_VMEM_CAP = 60000 * 1024


def _vmem_limit(block_bytes):
    return int(min(V7X_VMEM_CAP, block_bytes + (20 << 20)))


def _sigmoid(x):
    return 1.0 / (1.0 + jnp.exp(-x))


def _silu(x):
    return x * _sigmoid(x)


def _softplus(x):
    return jnp.maximum(x, 0.0) + jnp.log(1.0 + jnp.exp(-jnp.abs(x)))


def _dot(a, b):
    return jnp.dot(a, b, preferred_element_type=F32)


def _dot_nt(a, b):
    return lax.dot_general(a, b, (((1,), (1,)), ((), ())), preferred_element_type=F32)


def _dot_tn(a, b):
    return lax.dot_general(a, b, (((0,), (0,)), ((), ())), preferred_element_type=F32)


def _split3(x):
    hi = x.astype(BF16)
    r1 = x - hi.astype(F32)
    mid = r1.astype(BF16)
    lo = (r1 - mid.astype(F32)).astype(BF16)
    return hi, mid, lo


def _dot_exact_lhs(sel, x):
    hi, mid, lo = _split3(x)
    return _dot(sel, hi) + _dot(sel, mid) + _dot(sel, lo)


def _rmsnorm_kernel(x_ref, g_ref, *o_refs):
    x = x_ref[...]
    y = x * lax.rsqrt(jnp.mean(x * x, axis=-1, keepdims=True) + RMS_EPS)
    for n, o_ref in enumerate(o_refs):
        o_ref[...] = (y * g_ref[n:n + 1, :]).astype(o_ref.dtype)


def _rmsnorm(x, gains, rows=256):
    t, d = x.shape
    n = gains.shape[0]
    rows = min(rows, t)
    outs = pl.pallas_call(
        _rmsnorm_kernel,
        grid=(t // rows,),
        in_specs=[pl.BlockSpec((rows, d), lambda i: (i, 0)),
                  pl.BlockSpec((n, d), lambda i: (0, 0))],
        out_specs=[pl.BlockSpec((rows, d), lambda i: (i, 0))] * n,
        out_shape=[jax.ShapeDtypeStruct((t, d), BF16)] * n,
        compiler_params=pltpu.CompilerParams(
            dimension_semantics=("parallel",),
            vmem_limit_bytes=_vmem_limit(2 * rows * d * (4 + 2 * n))),
        name="rmsnorm",
    )(x, gains)
    return outs


def _cast_kernel(x_ref, o_ref):
    o_ref[...] = x_ref[...].astype(o_ref.dtype)


def _cast_bf16(w, layer, rows=256):
    _, k, n = w.shape
    rows = min(rows, k)
    assert k % rows == 0
    return pl.pallas_call(
        _cast_kernel,
        grid=(k // rows,),
        in_specs=[pl.BlockSpec((None, rows, n), lambda i: (layer, i, 0))],
        out_specs=pl.BlockSpec((rows, n), lambda i: (i, 0)),
        out_shape=jax.ShapeDtypeStruct((k, n), BF16),
        compiler_params=pltpu.CompilerParams(
            dimension_semantics=("parallel",),
            vmem_limit_bytes=_vmem_limit(2 * rows * n * 6)),
        name="cast_bf16",
    )(w)


def _ep_plain(accs, extra):
    return accs[0]


def _ep_resid(accs, extra):
    return accs[0] + extra[0][...]


def _ep_swiglu(accs, extra):
    g, u = accs
    return _silu(g) * u


def _ep_headnorm(accs, extra, scale):
    acc = accs[0]
    gain = extra[0][...] * scale
    outs = []
    for c in range(acc.shape[1] // HEAD_DIM):
        a = acc[:, c * HEAD_DIM:(c + 1) * HEAD_DIM]
        outs.append(a * lax.rsqrt(jnp.mean(a * a, axis=-1, keepdims=True) + RMS_EPS) * gain)
    return jnp.concatenate(outs, axis=1)


def _mm_stream_kernel(*refs, n_w, n_sub, n_extra, epilogue, w_valid, w_transposed, slab):
    n_ws = n_w * n_sub
    x_ref = refs[0]
    w_refs = refs[1:1 + n_ws]
    extra = refs[1 + n_ws:1 + n_ws + n_extra]
    o_ref = refs[1 + n_ws + n_extra]
    wb_refs = refs[2 + n_ws + n_extra:]
    phase = pl.program_id(0)
    i = pl.program_id(1)
    fill = lax.rem(phase, 2)
    for w_ref, wb_ref in zip(w_refs, wb_refs):
        w = w_ref[...]
        if w_valid is not None:
            n_axis = 0 if w_transposed else 1
            w = jnp.where(lax.broadcasted_iota(jnp.int32, w.shape, n_axis) < w_valid, w, 0.0)
        if w_transposed:
            w = w.T
        wb_ref[fill, pl.ds(pl.multiple_of(i * slab, slab), slab), :] = w.astype(BF16)

    @pl.when(phase >= 1)
    def _():
        x = x_ref[...]
        accs = [_dot(x, wb_ref[1 - fill]) for wb_ref in wb_refs]
        outs = [epilogue(accs[t * n_w:(t + 1) * n_w], extra) for t in range(n_sub)]
        out = outs[0] if n_sub == 1 else jnp.concatenate(outs, axis=1)
        o_ref[...] = out.astype(o_ref.dtype)


def _mm_resident_x_kernel(*refs, n_extra, epilogue):
    x_ref, w_ref = refs[0], refs[1]
    extra = refs[2:2 + n_extra]
    o_ref = refs[2 + n_extra]
    o_ref[...] = epilogue([_dot(x_ref[...], w_ref[...])], extra).astype(o_ref.dtype)


def _matmul(x, w, *, n_out, bm, bn, out_dtype, epilogue=_ep_plain, w_layer=None,
            w_col_offsets=(0,), extras=(), extra_kinds=(), w_valid=None, w_transposed=False,
            sub_blocks=1, name="matmul"):
    m, k = x.shape
    bm = min(bm, m)
    bn = min(bn, n_out)
    assert m % bm == 0 and n_out % bn == 0
    wide = sub_blocks * bn
    ni, nj = m // bm, pl.cdiv(n_out, wide)
    last_nb = n_out // bn - 1
    n_w = len(w_col_offsets)
    stream_w = w.dtype != BF16
    assert stream_w or (n_w == 1 and sub_blocks == 1 and w_valid is None and not w_transposed
                        and w_layer is None)
    assert sub_blocks == 1 or "tile" not in extra_kinds

    if stream_w:
        assert k % ni == 0
        slab = k // ni
        grid = (nj + 1, ni)
        row_of = lambda p, i: jnp.where(p == 0, 0, i)
        col_of = lambda p, i: jnp.maximum(p - 1, 0)
        sem = ("arbitrary", "arbitrary")
    else:
        grid = (ni, nj)
        row_of = lambda i, j: i
        col_of = lambda i, j: j
        sem = ("parallel", "arbitrary")

    in_specs = [pl.BlockSpec((bm, k), lambda a, b: (row_of(a, b), 0))]
    operands = [x]
    for t, off in [(t, off) for t in range(sub_blocks) for off in w_col_offsets]:
        assert off % bn == 0
        ob = off // bn
        nb = lambda p, t=t, ob=ob: jnp.minimum(p * sub_blocks + t, last_nb) + ob
        if not stream_w:
            in_specs.append(pl.BlockSpec((k, bn), lambda i, j: (0, j)))
        elif w_transposed:
            in_specs.append(pl.BlockSpec((None, bn, slab), lambda p, i, nb=nb: (w_layer, nb(p), i)))
        elif w_layer is None:
            in_specs.append(pl.BlockSpec((slab, bn), lambda p, i, nb=nb: (i, nb(p))))
        else:
            in_specs.append(pl.BlockSpec((None, slab, bn), lambda p, i, nb=nb: (w_layer, i, nb(p))))
        operands.append(w)
    extra_bytes = 0
    for e, kind in zip(extras, extra_kinds):
        if kind == "tile":
            in_specs.append(pl.BlockSpec((bm, wide), lambda a, b: (row_of(a, b), col_of(a, b))))
            extra_bytes += 2 * bm * wide * e.dtype.itemsize
        else:
            in_specs.append(pl.BlockSpec(e.shape, lambda a, b: (0, 0)))
        operands.append(e)
    out_bytes = 2 * bm * wide * jnp.dtype(out_dtype).itemsize
    if stream_w:
        n_ws = n_w * sub_blocks
        scratch = [pltpu.VMEM((2, k, bn), BF16) for _ in range(n_ws)]
        blk = 2 * bm * k * 2 + n_ws * (2 * slab * bn * 4 + 2 * k * bn * 2) + out_bytes + extra_bytes
        kern = functools.partial(_mm_stream_kernel, n_w=n_w, n_sub=sub_blocks, n_extra=len(extras),
                                 epilogue=epilogue, w_valid=w_valid, w_transposed=w_transposed, slab=slab)
    else:
        scratch = []
        blk = 2 * bm * k * 2 + 2 * k * bn * 2 + out_bytes + extra_bytes
        kern = functools.partial(_mm_resident_x_kernel, n_extra=len(extras), epilogue=epilogue)
    return pl.pallas_call(
        kern,
        grid=grid,
        in_specs=in_specs,
        out_specs=pl.BlockSpec((bm, wide), lambda a, b: (row_of(a, b), col_of(a, b))),
        out_shape=jax.ShapeDtypeStruct((m, n_out), out_dtype),
        scratch_shapes=scratch,
        compiler_params=pltpu.CompilerParams(
            dimension_semantics=sem, vmem_limit_bytes=_vmem_limit(blk)),
        name=name,
    )(*operands)


def _gdn_kernel(q_ref, k_ref, v_ref, z_ref, ba_ref, cq_ref, ck_ref, cv_ref, alog_ref, dtb_ref,
                og_ref, o_ref, state_ref, pq_ref, pk_ref, pv_ref, *, hb, n_heads):
    n = GDN_BLOCK
    c = pl.program_id(2)

    @pl.when(c == 0)
    def _():
        state_ref[...] = jnp.zeros_like(state_ref)
        for p_ref in (pq_ref, pk_ref, pv_ref):
            p_ref[...] = jnp.zeros_like(p_ref)

    row8 = lax.broadcasted_iota(jnp.int32, (SUBLANES, q_ref.shape[1]), 0)

    def conv_silu(x_ref, p_ref, w_ref):
        x = x_ref[...]
        prev = p_ref[...]
        y = x * w_ref[GDN_CONV - 1:GDN_CONV, :]
        for j in range(GDN_CONV - 1):
            sh = GDN_CONV - 1 - j
            rolled = pltpu.roll(x, sh, 0)
            first = jnp.where(row8 < sh, pltpu.roll(prev, sh, 0), rolled[0:SUBLANES, :])
            shifted = jnp.concatenate([first, rolled[SUBLANES:, :]], axis=0)
            y = y + shifted * w_ref[j:j + 1, :]
        p_ref[...] = x[n - SUBLANES:n, :]
        return _silu(y)

    qc = conv_silu(q_ref, pq_ref, cq_ref)
    kc = conv_silu(k_ref, pk_ref, ck_ref)
    vc = conv_silu(v_ref, pv_ref, cv_ref)

    row = lax.broadcasted_iota(jnp.int32, (n, n), 0)
    col = lax.broadcasted_iota(jnp.int32, (n, n), 1)
    tril = row >= col
    strict = row > col
    eye = row == col
    eye_f = jnp.where(eye, 1.0, 0.0).astype(F32)
    ltri = jnp.where(tril, 1.0, 0.0).astype(BF16)

    ba = ba_ref[...]
    gl = ba.shape[1]
    shift = lax.rem(gl - pl.program_id(1) * hb, gl)
    beta = pltpu.roll(_sigmoid(ba), shift, 1)
    g = -jnp.exp(alog_ref[...]) * _softplus(ba + dtb_ref[...])
    gcum = pltpu.roll(_dot_exact_lhs(ltri, g), shift, 1)
    gcum_t = gcum.T

    og = og_ref[...]
    heads = range(hb)
    sls = [slice(hh * HEAD_DIM, (hh + 1) * HEAD_DIM) for hh in heads]
    qs = [qc[:, sl] for sl in sls]
    ks = [kc[:, sl] for sl in sls]
    qs = [q * (lax.rsqrt(jnp.sum(q * q, axis=-1, keepdims=True) + L2_EPS) * HEAD_DIM ** -0.5) for q in qs]
    ks = [k * lax.rsqrt(jnp.sum(k * k, axis=-1, keepdims=True) + L2_EPS) for k in ks]
    gcs = [jnp.broadcast_to(gcum[:, n_heads + hh:n_heads + hh + 1], (n, HEAD_DIM)) for hh in heads]
    bts = [jnp.broadcast_to(beta[:, hh:hh + 1], (n, HEAD_DIM)) for hh in heads]
    grows = [gcum_t[n_heads + hh:n_heads + hh + 1, :] for hh in heads]
    kbs = [k * bt for k, bt in zip(ks, bts)]
    decs = [jnp.where(tril, jnp.exp(gc - grow), 0.0) for gc, grow in zip(gcs, grows)]
    kqs = [_dot_nt(jnp.concatenate([kb, q], axis=0).astype(BF16), k.astype(BF16))
           for kb, q, k in zip(kbs, qs, ks)]
    bms = [-jnp.where(strict, kq[:n] * dec, 0.0) for kq, dec in zip(kqs, decs)]
    intras = [(kq[n:] * dec).astype(BF16) for kq, dec in zip(kqs, decs)]
    tinvs = [eye_f + bm_ for bm_ in bms]
    for _ in range(int(math.log2(n)) - 1):
        bbs = [bm_.astype(BF16) for bm_ in bms]
        bms = [_dot(bb, bb) for bb in bbs]
        tinvs = [t + _dot(t.astype(BF16), bm_.astype(BF16)) for t, bm_ in zip(tinvs, bms)]
    egs = [jnp.exp(gc) for gc in gcs]
    uws = [_dot(t.astype(BF16), jnp.concatenate([vc[:, sl] * bt, kb * eg], axis=1).astype(BF16))
           for t, sl, bt, kb, eg in zip(tinvs, sls, bts, kbs, egs)]
    ss = [state_ref[hh] for hh in heads]
    wss = [_dot(jnp.concatenate([uw[:, HEAD_DIM:], q * eg], axis=0).astype(BF16), s.astype(BF16))
           for uw, q, eg, s in zip(uws, qs, egs, ss)]
    vns = [(uw[:, :HEAD_DIM] - ws[:n]).astype(BF16) for uw, ws in zip(uws, wss)]
    glasts = [gc[n - 1:n, :] for gc in gcs]
    kds = [(k * jnp.exp(glast - gc)).astype(BF16) for k, glast, gc in zip(ks, glasts, gcs)]
    for hh in heads:
        state_ref[hh] = ss[hh] * jnp.exp(glasts[hh]) + _dot_tn(kds[hh], vns[hh])
    os_ = [ws[n:] + _dot(intra, vn) for ws, intra, vn in zip(wss, intras, vns)]
    for hh in heads:
        o = os_[hh]
        o = o * lax.rsqrt(jnp.mean(o * o, axis=-1, keepdims=True) + RMS_EPS) * og
        o_ref[:, sls[hh]] = (o * _silu(z_ref[:, sls[hh]])).astype(o_ref.dtype)


def _gdn(proj, ba, conv_w, a_log_p, dt_bias_p, out_gain, *, batch, heads, hb):
    t = proj.shape[0]
    seq = t // batch
    n = GDN_BLOCK
    assert seq % n == 0 and heads % hb == 0
    nc = seq // n
    ng = heads // hb
    wd = hb * HEAD_DIM
    gl = ba.shape[1]

    def tile(part):
        return pl.BlockSpec((n, wd), lambda b, g, c, part=part: (b * nc + c, part * ng + g))

    def cw(part):
        return pl.BlockSpec((GDN_CONV, wd), lambda b, g, c, part=part: (0, part * ng + g))

    whole = lambda shape: pl.BlockSpec(shape, lambda b, g, c: (0, 0))
    blk = 2 * (4 * n * wd * 4 + n * gl * 4 + n * wd * 2) \
        + hb * HEAD_DIM * HEAD_DIM * 4 + 3 * SUBLANES * wd * 4
    return pl.pallas_call(
        functools.partial(_gdn_kernel, hb=hb, n_heads=heads),
        grid=(batch, ng, nc),
        in_specs=[tile(0), tile(1), tile(2), tile(3),
                  pl.BlockSpec((n, gl), lambda b, g, c: (b * nc + c, 0)),
                  cw(0), cw(1), cw(2),
                  whole((1, gl)), whole((1, gl)), whole((1, HEAD_DIM))],
        out_specs=pl.BlockSpec((n, wd), lambda b, g, c: (b * nc + c, g)),
        out_shape=jax.ShapeDtypeStruct((t, heads * HEAD_DIM), BF16),
        scratch_shapes=[pltpu.VMEM((hb, HEAD_DIM, HEAD_DIM), F32)]
        + [pltpu.VMEM((SUBLANES, wd), F32)] * 3,
        compiler_params=pltpu.CompilerParams(
            dimension_semantics=("parallel", "parallel", "arbitrary"),
            vmem_limit_bytes=_vmem_limit(blk)),
        name="gdn",
    )(proj, proj, proj, proj, ba, conv_w, conv_w, conv_w, a_log_p, dt_bias_p, out_gain)


def _attn_kernel(lam_ref, q_ref, k_ref, v_ref, sg_ref, o_ref, m_ref, l_ref, acc_ref, *, blk, row_split,
                 lam_init):
    qi = pl.program_id(2)
    d = HEAD_DIM
    m_ref[...] = jnp.full_like(m_ref, -jnp.inf)
    l_ref[...] = jnp.zeros_like(l_ref)
    acc_ref[...] = jnp.zeros_like(acc_ref)

    rows = blk // row_split

    def block(j, masked):
        start = pl.multiple_of(j * blk, blk)
        chains = [(mi, r) for r in range(row_split) for mi in range(2)]

        def nkeys(r):
            return (r + 1) * rows if masked else blk

        def rsl(r):
            return slice(r * rows, (r + 1) * rows)

        ss = [_dot_nt(q_ref[rsl(r), mi * d:(mi + 1) * d],
                      k_ref[pl.ds(start, nkeys(r)), mi * d:(mi + 1) * d]) for mi, r in chains]
        if masked:
            def mask(s, r):
                qc = (r * rows + lax.broadcasted_iota(jnp.int32, s.shape, 0)) // ATTN_CHUNK
                kc = lax.broadcasted_iota(jnp.int32, s.shape, 1) // ATTN_CHUNK
                return jnp.where(kc <= qc, s, -jnp.inf)
            ss = [mask(s, r) for s, (mi, r) in zip(ss, chains)]
        m_olds = [m_ref[mi, rsl(r), :] for mi, r in chains]
        m_news = [jnp.maximum(m_old, jnp.max(s, axis=-1, keepdims=True)) for m_old, s in zip(m_olds, ss)]
        ps = [jnp.exp2(s - jnp.concatenate([m_new] * (s.shape[1] // d), axis=1))
              for s, m_new in zip(ss, m_news)]
        alphas = [jnp.exp2(m_old - m_new) for m_old, m_new in zip(m_olds, m_news)]
        for (mi, r), alpha, p, m_new in zip(chains, alphas, ps, m_news):
            l_ref[mi, rsl(r), :] = alpha * l_ref[mi, rsl(r), :] + jnp.sum(p, axis=-1, keepdims=True)
            m_ref[mi, rsl(r), :] = m_new
        pvs = [_dot(p.astype(BF16), v_ref[pl.ds(start, nkeys(r)), :]) for p, (mi, r) in zip(ps, chains)]
        for (mi, r), alpha, pv in zip(chains, alphas, pvs):
            acc_ref[mi, rsl(r), :] = acc_ref[mi, rsl(r), :] * jnp.concatenate([alpha, alpha], axis=1) + pv

    def body(j, carry):
        block(j, False)
        return carry

    lax.fori_loop(0, qi, body, 0)
    block(qi, True)

    lp = lam_ref[...]
    lam = (jnp.exp(jnp.sum(lp[0:1] * lp[1:2], axis=-1, keepdims=True))
           - jnp.exp(jnp.sum(lp[2:3] * lp[3:4], axis=-1, keepdims=True)) + lam_init)
    o = (acc_ref[0] / jnp.concatenate([l_ref[0]] * 2, axis=1)
         - lam * (acc_ref[1] / jnp.concatenate([l_ref[1]] * 2, axis=1)))
    o = o * lax.rsqrt(jnp.mean(o * o, axis=-1, keepdims=True) + RMS_EPS)
    o_ref[...] = (o * sg_ref[...] * (1.0 - lam_init)).astype(o_ref.dtype)


def _diff_attention(q, k, v, lam_params, sub_gain, *, batch, heads, lam_init, blk=512, row_split=2):
    t = q.shape[0]
    seq = t // batch
    blk = min(blk, seq)
    assert seq % blk == 0 and blk % (row_split * ATTN_CHUNK) == 0
    nq = seq // blk
    hw = 2 * HEAD_DIM
    vmem = 2 * (2 * blk * hw * 2 + 2 * seq * hw * 2) + 4 * blk * HEAD_DIM * 4 + 2 * blk * hw * 4
    return pl.pallas_call(
        functools.partial(_attn_kernel, blk=blk, row_split=row_split, lam_init=lam_init),
        grid=(batch, heads, nq),
        in_specs=[pl.BlockSpec((4, HEAD_DIM), lambda b, h, qi: (0, 0)),
                  pl.BlockSpec((blk, hw), lambda b, h, qi: (b * nq + qi, h)),
                  pl.BlockSpec((seq, hw), lambda b, h, qi: (b, h)),
                  pl.BlockSpec((seq, hw), lambda b, h, qi: (b, h)),
                  pl.BlockSpec((1, hw), lambda b, h, qi: (0, 0))],
        out_specs=pl.BlockSpec((blk, hw), lambda b, h, qi: (b * nq + qi, h)),
        out_shape=jax.ShapeDtypeStruct((t, heads * hw), BF16),
        scratch_shapes=[pltpu.VMEM((2, blk, HEAD_DIM), F32),
                        pltpu.VMEM((2, blk, HEAD_DIM), F32),
                        pltpu.VMEM((2, blk, hw), F32)],
        compiler_params=pltpu.CompilerParams(
            dimension_semantics=("parallel", "parallel", "parallel"),
            vmem_limit_bytes=_vmem_limit(vmem)),
        name="diff_attn",
    )(lam_params, q, k, v, sub_gain)


BM = 1024
BN = 1024
BM_RESID, BN_RESID = 512, 1024
BN_FFN = 256
GDN_HEADS_PER_STEP = 16


def _gdn_layer(x2, i, batch, a_norm, a_w_in, a_conv, a_A_log, a_dt_bias, a_out_norm, a_w_out):
    d = x2.shape[1]
    heads = a_A_log.shape[1]
    hd = heads * HEAD_DIM
    gl = HEAD_DIM
    assert 2 * heads <= gl and a_w_in.shape[2] == 4 * hd + 2 * heads
    hb = min(GDN_HEADS_PER_STEP, heads)
    (hn,) = _rmsnorm(x2, a_norm[i:i + 1])
    w_in_t = jnp.swapaxes(a_w_in, 1, 2)
    proj = _matmul(hn, w_in_t, w_layer=i, n_out=4 * hd, bm=BM, bn=BN, out_dtype=F32,
                   w_transposed=True, name="gdn_in")
    ba = _matmul(hn, w_in_t, w_layer=i, n_out=gl, bm=BM, bn=gl, out_dtype=F32,
                 w_col_offsets=(4 * hd,), w_valid=2 * heads, w_transposed=True, name="gdn_in_gates")
    on_a_lanes = lambda p: jnp.pad(p[i:i + 1], ((0, 0), (heads, gl - 2 * heads)))
    og = _gdn(proj, ba, a_conv[i], on_a_lanes(a_A_log), on_a_lanes(a_dt_bias), a_out_norm[i:i + 1],
              batch=batch, heads=heads, hb=hb)
    return _matmul(og, a_w_out, w_layer=i, n_out=d, bm=BM_RESID, bn=BN_RESID, out_dtype=F32,
                   epilogue=_ep_resid, extras=(x2,), extra_kinds=("tile",), name="gdn_out")


def _ffn_layer(x2, layer, ffn_norm, ffn_w_gate_up, ffn_w_down):
    d = x2.shape[1]
    hidden = ffn_w_down.shape[1]
    (hn,) = _rmsnorm(x2, ffn_norm[layer:layer + 1])
    h = _matmul(hn, ffn_w_gate_up, w_layer=layer, n_out=hidden, bm=BM, bn=BN_FFN, out_dtype=BF16,
                epilogue=_ep_swiglu, w_col_offsets=(0, hidden), sub_blocks=2, name="ffn_gate_up")
    wd = _cast_bf16(ffn_w_down, layer)
    return _matmul(h, wd, n_out=d, bm=512, bn=512, out_dtype=F32, epilogue=_ep_resid,
                   extras=(x2,), extra_kinds=("tile",), name="ffn_down")


def kernel(x, a_norm, a_w_in, a_conv, a_A_log, a_dt_bias, a_out_norm, a_w_out, kv_norm, w_kv, k_norm,
           b_norm, b_w_q, b_q_norm, b_lambda, b_sub_norm, b_w_out, ffn_norm, ffn_w_gate_up, ffn_w_down):
    batch, seq, d = x.shape
    n_a = a_norm.shape[0]
    n_b = b_norm.shape[0]
    depth = n_a + n_b
    x2 = x.reshape(batch * seq, d)
    qk_width = b_w_q.shape[2]
    diff_heads = qk_width // (2 * HEAD_DIM)
    k_sh = v_sh = None
    for layer in range(depth):
        if layer < n_a:
            x2 = _gdn_layer(x2, layer, batch, a_norm, a_w_in, a_conv, a_A_log, a_dt_bias,
                            a_out_norm, a_w_out)
        else:
            j = layer - n_a
            if layer == n_a:
                kvn, hn = _rmsnorm(x2, jnp.stack([kv_norm, b_norm[j]]))
                k_sh = _matmul(kvn, w_kv, n_out=qk_width, bm=BM, bn=BN, out_dtype=BF16,
                               epilogue=functools.partial(_ep_headnorm, scale=1.0),
                               extras=(k_norm[None, :],), extra_kinds=("row",), name="kv_k")
                v_sh = _matmul(kvn, w_kv, n_out=w_kv.shape[1] - qk_width, bm=BM, bn=BN,
                               out_dtype=BF16, w_col_offsets=(qk_width,), name="kv_v")
            else:
                (hn,) = _rmsnorm(x2, b_norm[j:j + 1])
            lam_init = 0.8 - 0.6 * math.exp(-0.3 * layer)
            q = _matmul(hn, b_w_q, w_layer=j, n_out=qk_width, bm=BM, bn=BN, out_dtype=BF16,
                        epilogue=functools.partial(_ep_headnorm, scale=HEAD_DIM ** -0.5 * math.log2(math.e)),
                        extras=(b_q_norm[j:j + 1],), extra_kinds=("row",), name="attn_q")
            ao = _diff_attention(q, k_sh, v_sh, b_lambda[j], b_sub_norm[j:j + 1],
                                 batch=batch, heads=diff_heads, lam_init=lam_init)
            x2 = _matmul(ao, b_w_out, w_layer=j, n_out=d, bm=BM_RESID, bn=BN_RESID, out_dtype=F32,
                         epilogue=_ep_resid, extras=(x2,), extra_kinds=("tile",), name="attn_out")
        x2 = _ffn_layer(x2, layer, ffn_norm, ffn_w_gate_up, ffn_w_down)
    return x2.reshape(batch, seq, d)
```

```python
import functools
import math

import jax
import jax.numpy as jnp
from jax import lax
from jax.experimental import pallas as pl
from jax.experimental.pallas import tpu as pltpu

F32 = jnp.float32
BF16 = jnp.bfloat16

RMS_EPS = 1e-6
L2_EPS = 1e-6
HEAD_DIM = 128
ATTN_CHUNK = 64
GDN_BLOCK = 128
GDN_CONV = 4
SUBLANES = 8
V7X_VMEM_CAP = 60000 * 1024


def _vmem_limit(block_bytes):
    return int(min(V7X_VMEM_CAP, block_bytes + (20 << 20)))


def _sigmoid(x):
    return 1.0 / (1.0 + jnp.exp(-x))


def _silu(x):
    return x * _sigmoid(x)


def _softplus(x):
    return jnp.maximum(x, 0.0) + jnp.log(1.0 + jnp.exp(-jnp.abs(x)))


def _dot(a, b):
    return jnp.dot(a, b, preferred_element_type=F32)


def _dot_nt(a, b):
    return lax.dot_general(a, b, (((1,), (1,)), ((), ())), preferred_element_type=F32)


def _dot_tn(a, b):
    return lax.dot_general(a, b, (((0,), (0,)), ((), ())), preferred_element_type=F32)


def _split3(x):
    hi = x.astype(BF16)
    r1 = x - hi.astype(F32)
    mid = r1.astype(BF16)
    lo = (r1 - mid.astype(F32)).astype(BF16)
    return hi, mid, lo


def _dot_exact_lhs(sel, x):
    hi, mid, lo = _split3(x)
    return _dot(sel, hi) + _dot(sel, mid) + _dot(sel, lo)


def _rmsnorm_kernel(x_ref, g_ref, *o_refs):
    x = x_ref[...]
    y = x * lax.rsqrt(jnp.mean(x * x, axis=-1, keepdims=True) + RMS_EPS)
    for n, o_ref in enumerate(o_refs):
        o_ref[...] = (y * g_ref[n:n + 1, :]).astype(o_ref.dtype)


def _rmsnorm(x, gains, rows=256):
    t, d = x.shape
    n = gains.shape[0]
    rows = min(rows, t)
    outs = pl.pallas_call(
        _rmsnorm_kernel,
        grid=(t // rows,),
        in_specs=[pl.BlockSpec((rows, d), lambda i: (i, 0)),
                  pl.BlockSpec((n, d), lambda i: (0, 0))],
        out_specs=[pl.BlockSpec((rows, d), lambda i: (i, 0))] * n,
        out_shape=[jax.ShapeDtypeStruct((t, d), BF16)] * n,
        compiler_params=pltpu.CompilerParams(
            dimension_semantics=("parallel",),
            vmem_limit_bytes=_vmem_limit(2 * rows * d * (4 + 2 * n))),
        name="rmsnorm",
    )(x, gains)
    return outs


def _ep_plain(accs, extra):
    return accs[0]


def _ep_resid(accs, extra):
    return accs[0] + extra[0][...]


def _ep_resid_stats(accs, extra):
    y = accs[0] + extra[0][...]
    ssq = jnp.sum(y * y, axis=-1, keepdims=True)
    return y, y, jnp.broadcast_to(ssq, (y.shape[0], HEAD_DIM))


def _ep_swiglu(accs, extra):
    g, u = accs
    return _silu(g) * u


def _ep_headnorm(accs, extra, scale):
    acc = accs[0]
    gain = extra[0][...] * scale
    outs = []
    for c in range(acc.shape[1] // HEAD_DIM):
        a = acc[:, c * HEAD_DIM:(c + 1) * HEAD_DIM]
        outs.append(a * lax.rsqrt(jnp.mean(a * a, axis=-1, keepdims=True) + RMS_EPS) * gain)
    return jnp.concatenate(outs, axis=1)


def _mm_kernel(*refs, n_w, n_sub, n_extra, n_out_refs, epilogue, w_valid, w_transposed, slab, normed):
    n_ws = n_w * n_sub
    pos = 1 + n_ws
    x_ref = refs[0]
    w_refs = refs[1:pos]
    if normed:
        gain_ref, ssq_ref = refs[pos:pos + 2]
        pos += 2
    extra = refs[pos:pos + n_extra]
    o_refs = refs[pos + n_extra:pos + n_extra + n_out_refs]
    wb_refs = refs[pos + n_extra + n_out_refs:]
    phase = pl.program_id(0)
    i = pl.program_id(1)

    def fill_weights(fill):
        for w_ref, wb_ref in zip(w_refs, wb_refs):
            w = w_ref[...]
            if normed:
                w = w * gain_ref[...]
            if w_valid is not None:
                n_axis = 0 if w_transposed else 1
                w = jnp.where(lax.broadcasted_iota(jnp.int32, w.shape, n_axis) < w_valid, w, 0.0)
            if w_transposed:
                w = w.T
            wb_ref[fill, pl.ds(pl.multiple_of(i * slab, slab), slab), :] = w.astype(BF16)

    def step(fill):
        fill_weights(fill)
        x = x_ref[...]
        accs = [_dot(x, wb_ref[1 - fill]) for wb_ref in wb_refs]
        if normed:
            parts = ssq_ref[...]
            ssq = parts[:, 0:HEAD_DIM]
            for c in range(1, parts.shape[1] // HEAD_DIM):
                ssq = ssq + parts[:, c * HEAD_DIM:(c + 1) * HEAD_DIM]
            r = lax.rsqrt(ssq * (1.0 / x.shape[1]) + RMS_EPS)
            accs = [a * jnp.concatenate([r] * (a.shape[1] // HEAD_DIM), axis=1) for a in accs]
        outs = [epilogue(accs[t * n_w:(t + 1) * n_w], extra) for t in range(n_sub)]
        if n_out_refs == 1:
            outs = [(o,) for o in outs]
        for k, o_ref in enumerate(o_refs):
            vals = [o[k] for o in outs]
            val = vals[0] if n_sub == 1 else jnp.concatenate(vals, axis=1)
            o_ref[...] = val.astype(o_ref.dtype)

    odd = lax.rem(phase, 2) == 1
    pl.when(phase == 0)(lambda: fill_weights(0))
    pl.when(jnp.logical_and(phase >= 1, jnp.logical_not(odd)))(lambda: step(0))
    pl.when(odd)(lambda: step(1))


def _matmul(x, w, *, n_out, bm, bn, out_dtype, epilogue=_ep_plain, w_layer=None,
            w_col_offsets=(0,), extras=(), extra_kinds=(), w_valid=None, w_transposed=False,
            sub_blocks=1, norm=None, with_stats=False, name="matmul"):
    m, k = x.shape
    bm = min(bm, m)
    bn = min(bn, n_out)
    assert m % bm == 0 and n_out % bn == 0 and k % (m // bm) == 0
    wide = sub_blocks * bn
    ni, nj = m // bm, pl.cdiv(n_out, wide)
    slab = k // ni
    last_nb = n_out // bn - 1
    n_w = len(w_col_offsets)
    assert sub_blocks == 1 or ("tile" not in extra_kinds and not with_stats)
    row_of = lambda p, i: jnp.where(p == 0, 0, i)
    col_of = lambda p, i: jnp.maximum(p - 1, 0)

    in_specs = [pl.BlockSpec((bm, k), lambda p, i: (row_of(p, i), 0))]
    operands = [x]
    for t, off in [(t, off) for t in range(sub_blocks) for off in w_col_offsets]:
        assert off % bn == 0
        ob = off // bn
        nb = lambda p, t=t, ob=ob: jnp.minimum(p * sub_blocks + t, last_nb) + ob
        if w_transposed:
            in_specs.append(pl.BlockSpec((None, bn, slab), lambda p, i, nb=nb: (w_layer, nb(p), i)))
        elif w_layer is None:
            in_specs.append(pl.BlockSpec((slab, bn), lambda p, i, nb=nb: (i, nb(p))))
        else:
            in_specs.append(pl.BlockSpec((None, slab, bn), lambda p, i, nb=nb: (w_layer, i, nb(p))))
        operands.append(w)
    extra_bytes = 0
    if norm is not None:
        gain, ssq = norm
        if w_transposed:
            in_specs.append(pl.BlockSpec((1, slab), lambda p, i: (0, i)))
            operands.append(gain.reshape(1, k))
        else:
            in_specs.append(pl.BlockSpec((slab, 1), lambda p, i: (i, 0)))
            operands.append(gain.reshape(k, 1))
        in_specs.append(pl.BlockSpec((bm, ssq.shape[1]), lambda p, i: (row_of(p, i), 0)))
        operands.append(ssq)
        extra_bytes += 2 * (slab * HEAD_DIM * 4 + bm * ssq.shape[1] * 4)
    for e, kind in zip(extras, extra_kinds):
        if kind == "tile":
            in_specs.append(pl.BlockSpec((bm, wide), lambda p, i: (row_of(p, i), col_of(p, i))))
            extra_bytes += 2 * bm * wide * e.dtype.itemsize
        else:
            in_specs.append(pl.BlockSpec(e.shape, lambda p, i: (0, 0)))
        operands.append(e)
    tile_spec = pl.BlockSpec((bm, wide), lambda p, i: (row_of(p, i), col_of(p, i)))
    if with_stats:
        out_specs = [tile_spec, tile_spec,
                     pl.BlockSpec((bm, HEAD_DIM), lambda p, i: (row_of(p, i), col_of(p, i)))]
        out_shape = [jax.ShapeDtypeStruct((m, n_out), out_dtype), jax.ShapeDtypeStruct((m, n_out), BF16),
                     jax.ShapeDtypeStruct((m, nj * HEAD_DIM), F32)]
        out_bytes = 2 * bm * (wide * (jnp.dtype(out_dtype).itemsize + 2) + HEAD_DIM * 4)
    else:
        out_specs = tile_spec
        out_shape = jax.ShapeDtypeStruct((m, n_out), out_dtype)
        out_bytes = 2 * bm * wide * jnp.dtype(out_dtype).itemsize
    n_ws = n_w * sub_blocks
    blk = 2 * bm * k * 2 + n_ws * (2 * slab * bn * 4 + 2 * k * bn * 2) + out_bytes + extra_bytes
    kern = functools.partial(_mm_kernel, n_w=n_w, n_sub=sub_blocks, n_extra=len(extras),
                             n_out_refs=3 if with_stats else 1, epilogue=epilogue, w_valid=w_valid,
                             w_transposed=w_transposed, slab=slab, normed=norm is not None)
    return pl.pallas_call(
        kern,
        grid=(nj + 1, ni),
        in_specs=in_specs,
        out_specs=out_specs,
        out_shape=out_shape,
        scratch_shapes=[pltpu.VMEM((2, k, bn), BF16) for _ in range(n_ws)],
        compiler_params=pltpu.CompilerParams(
            dimension_semantics=("arbitrary", "arbitrary"), vmem_limit_bytes=_vmem_limit(blk)),
        name=name,
    )(*operands)


def _gdn_kernel(q_ref, k_ref, v_ref, z_ref, ba_ref, cq_ref, ck_ref, cv_ref, alog_ref, dtb_ref,
                og_ref, o_ref, state_ref, pq_ref, pk_ref, pv_ref, *, hb, n_heads):
    n = GDN_BLOCK
    c = pl.program_id(2)

    @pl.when(c == 0)
    def _():
        state_ref[...] = jnp.zeros_like(state_ref)
        for p_ref in (pq_ref, pk_ref, pv_ref):
            p_ref[...] = jnp.zeros_like(p_ref)

    row8 = lax.broadcasted_iota(jnp.int32, (SUBLANES, q_ref.shape[1]), 0)

    def conv_silu(x_ref, p_ref, w_ref):
        x = x_ref[...]
        prev = p_ref[...]
        y = x * w_ref[GDN_CONV - 1:GDN_CONV, :]
        for j in range(GDN_CONV - 1):
            sh = GDN_CONV - 1 - j
            rolled = pltpu.roll(x, sh, 0)
            first = jnp.where(row8 < sh, pltpu.roll(prev, sh, 0), rolled[0:SUBLANES, :])
            shifted = jnp.concatenate([first, rolled[SUBLANES:, :]], axis=0)
            y = y + shifted * w_ref[j:j + 1, :]
        p_ref[...] = x[n - SUBLANES:n, :]
        return _silu(y)

    qc = conv_silu(q_ref, pq_ref, cq_ref)
    kc = conv_silu(k_ref, pk_ref, ck_ref)
    vc = conv_silu(v_ref, pv_ref, cv_ref)

    row = lax.broadcasted_iota(jnp.int32, (n, n), 0)
    col = lax.broadcasted_iota(jnp.int32, (n, n), 1)
    tril = row >= col
    strict = row > col
    eye = row == col
    eye_f = jnp.where(eye, 1.0, 0.0).astype(F32)
    ltri = jnp.where(tril, 1.0, 0.0).astype(BF16)

    ba = ba_ref[...]
    gl = ba.shape[1]
    shift = lax.rem(gl - pl.program_id(1) * hb, gl)
    beta = pltpu.roll(_sigmoid(ba), shift, 1)
    g = -jnp.exp(alog_ref[...]) * _softplus(ba + dtb_ref[...])
    gcum = pltpu.roll(_dot_exact_lhs(ltri, g), shift, 1)
    gcum_t = gcum.T

    og = og_ref[...]
    heads = range(hb)
    sls = [slice(hh * HEAD_DIM, (hh + 1) * HEAD_DIM) for hh in heads]
    qs = [qc[:, sl] for sl in sls]
    ks = [kc[:, sl] for sl in sls]
    qs = [q * (lax.rsqrt(jnp.sum(q * q, axis=-1, keepdims=True) + L2_EPS) * HEAD_DIM ** -0.5) for q in qs]
    ks = [k * lax.rsqrt(jnp.sum(k * k, axis=-1, keepdims=True) + L2_EPS) for k in ks]
    gcs = [jnp.broadcast_to(gcum[:, n_heads + hh:n_heads + hh + 1], (n, HEAD_DIM)) for hh in heads]
    bts = [jnp.broadcast_to(beta[:, hh:hh + 1], (n, HEAD_DIM)) for hh in heads]
    grows = [gcum_t[n_heads + hh:n_heads + hh + 1, :] for hh in heads]
    kbs = [k * bt for k, bt in zip(ks, bts)]
    decs = [jnp.where(tril, jnp.exp(gc - grow), 0.0) for gc, grow in zip(gcs, grows)]
    kqs = [_dot_nt(jnp.concatenate([kb, q], axis=0).astype(BF16), k.astype(BF16))
           for kb, q, k in zip(kbs, qs, ks)]
    bms = [-jnp.where(strict, kq[:n] * dec, 0.0) for kq, dec in zip(kqs, decs)]
    intras = [(kq[n:] * dec).astype(BF16) for kq, dec in zip(kqs, decs)]
    tinvs = [eye_f + bm_ for bm_ in bms]
    for _ in range(int(math.log2(n)) - 1):
        bbs = [bm_.astype(BF16) for bm_ in bms]
        bms = [_dot(bb, bb) for bb in bbs]
        tinvs = [t + _dot(t.astype(BF16), bm_.astype(BF16)) for t, bm_ in zip(tinvs, bms)]
    egs = [jnp.exp(gc) for gc in gcs]
    uws = [_dot(t.astype(BF16), jnp.concatenate([vc[:, sl] * bt, kb * eg], axis=1).astype(BF16))
           for t, sl, bt, kb, eg in zip(tinvs, sls, bts, kbs, egs)]
    ss = [state_ref[hh] for hh in heads]
    wss = [_dot(jnp.concatenate([uw[:, HEAD_DIM:], q * eg], axis=0).astype(BF16), s.astype(BF16))
           for uw, q, eg, s in zip(uws, qs, egs, ss)]
    vns = [(uw[:, :HEAD_DIM] - ws[:n]).astype(BF16) for uw, ws in zip(uws, wss)]
    glasts = [gc[n - 1:n, :] for gc in gcs]
    kds = [(k * jnp.exp(glast - gc)).astype(BF16) for k, glast, gc in zip(ks, glasts, gcs)]
    for hh in heads:
        state_ref[hh] = ss[hh] * jnp.exp(glasts[hh]) + _dot_tn(kds[hh], vns[hh])
    os_ = [ws[n:] + _dot(intra, vn) for ws, intra, vn in zip(wss, intras, vns)]
    for hh in heads:
        o = os_[hh]
        o = o * lax.rsqrt(jnp.mean(o * o, axis=-1, keepdims=True) + RMS_EPS) * og
        o_ref[:, sls[hh]] = (o * _silu(z_ref[:, sls[hh]])).astype(o_ref.dtype)


def _gdn(proj, ba, conv_w, a_log_p, dt_bias_p, out_gain, *, batch, heads, hb):
    t = proj.shape[0]
    seq = t // batch
    n = GDN_BLOCK
    assert seq % n == 0 and heads % hb == 0
    nc = seq // n
    ng = heads // hb
    wd = hb * HEAD_DIM
    gl = ba.shape[1]

    def tile(part):
        return pl.BlockSpec((n, wd), lambda b, g, c, part=part: (b * nc + c, part * ng + g))

    def cw(part):
        return pl.BlockSpec((GDN_CONV, wd), lambda b, g, c, part=part: (0, part * ng + g))

    whole = lambda shape: pl.BlockSpec(shape, lambda b, g, c: (0, 0))
    blk = 2 * (4 * n * wd * 4 + n * gl * 4 + n * wd * 2) \
        + hb * HEAD_DIM * HEAD_DIM * 4 + 3 * SUBLANES * wd * 4
    return pl.pallas_call(
        functools.partial(_gdn_kernel, hb=hb, n_heads=heads),
        grid=(batch, ng, nc),
        in_specs=[tile(0), tile(1), tile(2), tile(3),
                  pl.BlockSpec((n, gl), lambda b, g, c: (b * nc + c, 0)),
                  cw(0), cw(1), cw(2),
                  whole((1, gl)), whole((1, gl)), whole((1, HEAD_DIM))],
        out_specs=pl.BlockSpec((n, wd), lambda b, g, c: (b * nc + c, g)),
        out_shape=jax.ShapeDtypeStruct((t, heads * HEAD_DIM), BF16),
        scratch_shapes=[pltpu.VMEM((hb, HEAD_DIM, HEAD_DIM), F32)]
        + [pltpu.VMEM((SUBLANES, wd), F32)] * 3,
        compiler_params=pltpu.CompilerParams(
            dimension_semantics=("parallel", "parallel", "arbitrary"),
            vmem_limit_bytes=_vmem_limit(blk)),
        name="gdn",
    )(proj, proj, proj, proj, ba, conv_w, conv_w, conv_w, a_log_p, dt_bias_p, out_gain)


def _attn_kernel(lam_ref, q_ref, k_ref, v_ref, sg_ref, o_ref, m_ref, l_ref, acc_ref, *, blk, row_split,
                 lam_init):
    qi = pl.program_id(2)
    d = HEAD_DIM
    m_ref[...] = jnp.full_like(m_ref, -jnp.inf)
    l_ref[...] = jnp.zeros_like(l_ref)
    acc_ref[...] = jnp.zeros_like(acc_ref)

    rows = blk // row_split

    def block(j, masked):
        start = pl.multiple_of(j * blk, blk)
        chains = [(mi, r) for r in range(row_split) for mi in range(2)]

        def nkeys(r):
            return (r + 1) * rows if masked else blk

        def rsl(r):
            return slice(r * rows, (r + 1) * rows)

        ss = [_dot_nt(q_ref[rsl(r), mi * d:(mi + 1) * d],
                      k_ref[pl.ds(start, nkeys(r)), mi * d:(mi + 1) * d]) for mi, r in chains]
        if masked:
            def mask(s, r):
                qc = (r * rows + lax.broadcasted_iota(jnp.int32, s.shape, 0)) // ATTN_CHUNK
                kc = lax.broadcasted_iota(jnp.int32, s.shape, 1) // ATTN_CHUNK
                return jnp.where(kc <= qc, s, -jnp.inf)
            ss = [mask(s, r) for s, (mi, r) in zip(ss, chains)]
        m_olds = [m_ref[mi, rsl(r), :] for mi, r in chains]
        m_news = [jnp.maximum(m_old, jnp.max(s, axis=-1, keepdims=True)) for m_old, s in zip(m_olds, ss)]
        ps = [jnp.exp2(s - jnp.concatenate([m_new] * (s.shape[1] // d), axis=1))
              for s, m_new in zip(ss, m_news)]
        alphas = [jnp.exp2(m_old - m_new) for m_old, m_new in zip(m_olds, m_news)]
        for (mi, r), alpha, p, m_new in zip(chains, alphas, ps, m_news):
            l_ref[mi, rsl(r), :] = alpha * l_ref[mi, rsl(r), :] + jnp.sum(p, axis=-1, keepdims=True)
            m_ref[mi, rsl(r), :] = m_new
        pvs = [_dot(p.astype(BF16), v_ref[pl.ds(start, nkeys(r)), :]) for p, (mi, r) in zip(ps, chains)]
        for (mi, r), alpha, pv in zip(chains, alphas, pvs):
            acc_ref[mi, rsl(r), :] = acc_ref[mi, rsl(r), :] * jnp.concatenate([alpha, alpha], axis=1) + pv

    def body(j, carry):
        block(j, False)
        return carry

    lax.fori_loop(0, qi, body, 0)
    block(qi, True)

    lp = lam_ref[...]
    lam = (jnp.exp(jnp.sum(lp[0:1] * lp[1:2], axis=-1, keepdims=True))
           - jnp.exp(jnp.sum(lp[2:3] * lp[3:4], axis=-1, keepdims=True)) + lam_init)
    o = (acc_ref[0] / jnp.concatenate([l_ref[0]] * 2, axis=1)
         - lam * (acc_ref[1] / jnp.concatenate([l_ref[1]] * 2, axis=1)))
    o = o * lax.rsqrt(jnp.mean(o * o, axis=-1, keepdims=True) + RMS_EPS)
    o_ref[...] = (o * sg_ref[...] * (1.0 - lam_init)).astype(o_ref.dtype)


def _diff_attention(q, k, v, lam_params, sub_gain, *, batch, heads, lam_init, blk=512, row_split=2):
    t = q.shape[0]
    seq = t // batch
    blk = min(blk, seq)
    assert seq % blk == 0 and blk % (row_split * ATTN_CHUNK) == 0
    nq = seq // blk
    hw = 2 * HEAD_DIM
    vmem = 2 * (2 * blk * hw * 2 + 2 * seq * hw * 2) + 4 * blk * HEAD_DIM * 4 + 2 * blk * hw * 4
    return pl.pallas_call(
        functools.partial(_attn_kernel, blk=blk, row_split=row_split, lam_init=lam_init),
        grid=(batch, heads, nq),
        in_specs=[pl.BlockSpec((4, HEAD_DIM), lambda b, h, qi: (0, 0)),
                  pl.BlockSpec((blk, hw), lambda b, h, qi: (b * nq + qi, h)),
                  pl.BlockSpec((seq, hw), lambda b, h, qi: (b, h)),
                  pl.BlockSpec((seq, hw), lambda b, h, qi: (b, h)),
                  pl.BlockSpec((1, hw), lambda b, h, qi: (0, 0))],
        out_specs=pl.BlockSpec((blk, hw), lambda b, h, qi: (b * nq + qi, h)),
        out_shape=jax.ShapeDtypeStruct((t, heads * hw), BF16),
        scratch_shapes=[pltpu.VMEM((2, blk, HEAD_DIM), F32),
                        pltpu.VMEM((2, blk, HEAD_DIM), F32),
                        pltpu.VMEM((2, blk, hw), F32)],
        compiler_params=pltpu.CompilerParams(
            dimension_semantics=("parallel", "parallel", "parallel"),
            vmem_limit_bytes=_vmem_limit(vmem)),
        name="diff_attn",
    )(lam_params, q, k, v, sub_gain)


BM = 1024
BN = 1024
BM_RESID, BN_RESID = 512, 1024
BN_FFN = 256
GDN_HEADS_PER_STEP = 16


def _normed_input(stream, gain):
    x2, xb, ssq = stream
    if xb is None:
        (hn,) = _rmsnorm(x2, gain[None, :])
        return hn, None
    return xb, (gain, ssq)


def _resid_matmul(a, w, layer, stream, last, name, bm=BM_RESID, bn=BN_RESID):
    x2 = stream[0]
    d = x2.shape[1]
    if last:
        y = _matmul(a, w, w_layer=layer, n_out=d, bm=bm, bn=bn, out_dtype=F32, epilogue=_ep_resid,
                    extras=(x2,), extra_kinds=("tile",), name=name)
        return y, None, None
    return _matmul(a, w, w_layer=layer, n_out=d, bm=bm, bn=bn, out_dtype=F32, epilogue=_ep_resid_stats,
                   extras=(x2,), extra_kinds=("tile",), with_stats=True, name=name)


def _gdn_layer(stream, i, batch, a_norm, a_w_in, a_conv, a_A_log, a_dt_bias, a_out_norm, a_w_out):
    heads = a_A_log.shape[1]
    hd = heads * HEAD_DIM
    gl = HEAD_DIM
    assert 2 * heads <= gl and a_w_in.shape[2] == 4 * hd + 2 * heads
    hb = min(GDN_HEADS_PER_STEP, heads)
    hn, norm = _normed_input(stream, a_norm[i])
    w_in_t = jnp.swapaxes(a_w_in, 1, 2)
    proj = _matmul(hn, w_in_t, w_layer=i, n_out=4 * hd, bm=BM, bn=BN, out_dtype=F32,
                   w_transposed=True, norm=norm, name="gdn_in")
    ba = _matmul(hn, w_in_t, w_layer=i, n_out=gl, bm=BM, bn=gl, out_dtype=F32, w_col_offsets=(4 * hd,),
                 w_valid=2 * heads, w_transposed=True, norm=norm, name="gdn_in_gates")
    on_a_lanes = lambda p: jnp.pad(p[i:i + 1], ((0, 0), (heads, gl - 2 * heads)))
    og = _gdn(proj, ba, a_conv[i], on_a_lanes(a_A_log), on_a_lanes(a_dt_bias), a_out_norm[i:i + 1],
              batch=batch, heads=heads, hb=hb)
    return _resid_matmul(og, a_w_out, i, stream, False, "gdn_out")


def _ffn_layer(stream, layer, last, ffn_norm, ffn_w_gate_up, ffn_w_down):
    hidden = ffn_w_down.shape[1]
    hn, norm = _normed_input(stream, ffn_norm[layer])
    h = _matmul(hn, ffn_w_gate_up, w_layer=layer, n_out=hidden, bm=BM, bn=BN_FFN, out_dtype=BF16,
                epilogue=_ep_swiglu, w_col_offsets=(0, hidden), sub_blocks=2, norm=norm,
                name="ffn_gate_up")
    return _resid_matmul(h, ffn_w_down, layer, stream, last, "ffn_down", bm=512, bn=512)


def kernel(x, a_norm, a_w_in, a_conv, a_A_log, a_dt_bias, a_out_norm, a_w_out, kv_norm, w_kv, k_norm,
           b_norm, b_w_q, b_q_norm, b_lambda, b_sub_norm, b_w_out, ffn_norm, ffn_w_gate_up, ffn_w_down):
    batch, seq, d = x.shape
    n_a = a_norm.shape[0]
    n_b = b_norm.shape[0]
    depth = n_a + n_b
    stream = (x.reshape(batch * seq, d), None, None)
    qk_width = b_w_q.shape[2]
    diff_heads = qk_width // (2 * HEAD_DIM)
    k_sh = v_sh = None
    for layer in range(depth):
        if layer < n_a:
            stream = _gdn_layer(stream, layer, batch, a_norm, a_w_in, a_conv, a_A_log, a_dt_bias,
                                a_out_norm, a_w_out)
        else:
            j = layer - n_a
            if layer == n_a:
                kvn, norm = _normed_input(stream, kv_norm)
                k_sh = _matmul(kvn, w_kv, n_out=qk_width, bm=BM, bn=BN, out_dtype=BF16,
                               epilogue=functools.partial(_ep_headnorm, scale=1.0),
                               extras=(k_norm[None, :],), extra_kinds=("row",), norm=norm, name="kv_k")
                v_sh = _matmul(kvn, w_kv, n_out=w_kv.shape[1] - qk_width, bm=BM, bn=BN, out_dtype=BF16,
                               w_col_offsets=(qk_width,), norm=norm, name="kv_v")
            hn, norm = _normed_input(stream, b_norm[j])
            lam_init = 0.8 - 0.6 * math.exp(-0.3 * layer)
            q = _matmul(hn, b_w_q, w_layer=j, n_out=qk_width, bm=BM, bn=BN, out_dtype=BF16,
                        epilogue=functools.partial(_ep_headnorm, scale=HEAD_DIM ** -0.5 * math.log2(math.e)),
                        extras=(b_q_norm[j:j + 1],), extra_kinds=("row",), norm=norm, name="attn_q")
            ao = _diff_attention(q, k_sh, v_sh, b_lambda[j], b_sub_norm[j:j + 1],
                                 batch=batch, heads=diff_heads, lam_init=lam_init)
            stream = _resid_matmul(ao, b_w_out, j, stream, False, "attn_out")
        stream = _ffn_layer(stream, layer, layer == depth - 1, ffn_norm, ffn_w_gate_up, ffn_w_down)
    return stream[0].reshape(batch, seq, d)
```

```python
import functools
import math

import jax
import jax.numpy as jnp
from jax import lax
from jax.experimental import pallas as pl
from jax.experimental.pallas import tpu as pltpu

F32 = jnp.float32
BF16 = jnp.bfloat16

RMS_EPS = 1e-6
L2_EPS = 1e-6
HEAD_DIM = 128
ATTN_CHUNK = 64
GDN_BLOCK = 128
GDN_CONV = 4
SUBLANES = 8
V7X_VMEM_CAP = 60000 * 1024


def _vmem_limit(block_bytes):
    return int(min(V7X_VMEM_CAP, block_bytes + (20 << 20)))


def _sigmoid(x):
    return 1.0 / (1.0 + jnp.exp(-x))


def _silu(x):
    return x * _sigmoid(x)


def _softplus(x):
    return jnp.maximum(x, 0.0) + jnp.log(1.0 + jnp.exp(-jnp.abs(x)))


def _dot(a, b):
    return jnp.dot(a, b, preferred_element_type=F32)


def _dot_nt(a, b):
    return lax.dot_general(a, b, (((1,), (1,)), ((), ())), preferred_element_type=F32)


def _dot_tn(a, b):
    return lax.dot_general(a, b, (((0,), (0,)), ((), ())), preferred_element_type=F32)


def _split3(x):
    hi = x.astype(BF16)
    r1 = x - hi.astype(F32)
    mid = r1.astype(BF16)
    lo = (r1 - mid.astype(F32)).astype(BF16)
    return hi, mid, lo


def _dot_exact_lhs(sel, x):
    hi, mid, lo = _split3(x)
    return _dot(sel, hi) + _dot(sel, mid) + _dot(sel, lo)


def _rmsnorm_kernel(x_ref, g_ref, *o_refs):
    x = x_ref[...]
    y = x * lax.rsqrt(jnp.mean(x * x, axis=-1, keepdims=True) + RMS_EPS)
    for n, o_ref in enumerate(o_refs):
        o_ref[...] = (y * g_ref[n:n + 1, :]).astype(o_ref.dtype)


def _rmsnorm(x, gains, rows=256):
    t, d = x.shape
    n = gains.shape[0]
    rows = min(rows, t)
    outs = pl.pallas_call(
        _rmsnorm_kernel,
        grid=(t // rows,),
        in_specs=[pl.BlockSpec((rows, d), lambda i: (i, 0)),
                  pl.BlockSpec((n, d), lambda i: (0, 0))],
        out_specs=[pl.BlockSpec((rows, d), lambda i: (i, 0))] * n,
        out_shape=[jax.ShapeDtypeStruct((t, d), BF16)] * n,
        compiler_params=pltpu.CompilerParams(
            dimension_semantics=("parallel",),
            vmem_limit_bytes=_vmem_limit(2 * rows * d * (4 + 2 * n))),
        name="rmsnorm",
    )(x, gains)
    return outs


def _ep_plain(accs, extra):
    return accs[0]


def _ep_resid(accs, extra):
    return accs[0] + extra[0][...]


def _ep_resid_stats(accs, extra):
    y = accs[0] + extra[0][...]
    ssq = jnp.sum(y * y, axis=-1, keepdims=True)
    return y, y, jnp.broadcast_to(ssq, (y.shape[0], HEAD_DIM))


def _ep_swiglu(accs, extra):
    g, u = accs
    return _silu(g) * u


def _ep_headnorm(accs, extra, scale):
    acc = accs[0]
    gain = extra[0][...] * scale
    outs = []
    for c in range(acc.shape[1] // HEAD_DIM):
        a = acc[:, c * HEAD_DIM:(c + 1) * HEAD_DIM]
        outs.append(a * lax.rsqrt(jnp.mean(a * a, axis=-1, keepdims=True) + RMS_EPS) * gain)
    return jnp.concatenate(outs, axis=1)


def _mm_kernel(*refs, n_w, n_sub, n_extra, n_out_refs, epilogue, w_valid, w_transposed, slab, normed):
    n_ws = n_w * n_sub
    pos = 1 + n_ws
    x_ref = refs[0]
    w_refs = refs[1:pos]
    if normed:
        gain_ref, ssq_ref = refs[pos:pos + 2]
        pos += 2
    extra = refs[pos:pos + n_extra]
    o_refs = refs[pos + n_extra:pos + n_extra + n_out_refs]
    wb_refs = refs[pos + n_extra + n_out_refs:]
    phase = pl.program_id(0)
    i = pl.program_id(1)

    def fill_weights(fill):
        for w_ref, wb_ref in zip(w_refs, wb_refs):
            w = w_ref[...]
            if normed:
                g = gain_ref[...]
                if not w_transposed:
                    g = jnp.concatenate([g] * (w.shape[1] // HEAD_DIM), axis=1)
                w = w * g
            if w_valid is not None:
                n_axis = 0 if w_transposed else 1
                w = jnp.where(lax.broadcasted_iota(jnp.int32, w.shape, n_axis) < w_valid, w, 0.0)
            if w_transposed:
                w = w.T
            wb_ref[fill, pl.ds(pl.multiple_of(i * slab, slab), slab), :] = w.astype(BF16)

    def step(fill):
        fill_weights(fill)
        x = x_ref[...]
        accs = [_dot(x, wb_ref[1 - fill]) for wb_ref in wb_refs]
        if normed:
            parts = ssq_ref[...]
            ssq = parts[:, 0:HEAD_DIM]
            for c in range(1, parts.shape[1] // HEAD_DIM):
                ssq = ssq + parts[:, c * HEAD_DIM:(c + 1) * HEAD_DIM]
            r = lax.rsqrt(ssq * (1.0 / x.shape[1]) + RMS_EPS)
            accs = [a * jnp.concatenate([r] * (a.shape[1] // HEAD_DIM), axis=1) for a in accs]
        outs = [epilogue(accs[t * n_w:(t + 1) * n_w], extra) for t in range(n_sub)]
        if n_out_refs == 1:
            outs = [(o,) for o in outs]
        for k, o_ref in enumerate(o_refs):
            vals = [o[k] for o in outs]
            val = vals[0] if n_sub == 1 else jnp.concatenate(vals, axis=1)
            o_ref[...] = val.astype(o_ref.dtype)

    odd = lax.rem(phase, 2) == 1
    pl.when(phase == 0)(lambda: fill_weights(0))
    pl.when(jnp.logical_and(phase >= 1, jnp.logical_not(odd)))(lambda: step(0))
    pl.when(odd)(lambda: step(1))


def _matmul(x, w, *, n_out, bm, bn, out_dtype, epilogue=_ep_plain, w_layer=None,
            w_col_offsets=(0,), extras=(), extra_kinds=(), w_valid=None, w_transposed=False,
            sub_blocks=1, norm=None, with_stats=False, name="matmul"):
    m, k = x.shape
    bm = min(bm, m)
    bn = min(bn, n_out)
    assert m % bm == 0 and n_out % bn == 0 and k % (m // bm) == 0
    wide = sub_blocks * bn
    ni, nj = m // bm, pl.cdiv(n_out, wide)
    slab = k // ni
    last_nb = n_out // bn - 1
    n_w = len(w_col_offsets)
    assert sub_blocks == 1 or ("tile" not in extra_kinds and not with_stats)
    row_of = lambda p, i: jnp.where(p == 0, 0, i)
    col_of = lambda p, i: jnp.maximum(p - 1, 0)

    in_specs = [pl.BlockSpec((bm, k), lambda p, i: (row_of(p, i), 0))]
    operands = [x]
    for t, off in [(t, off) for t in range(sub_blocks) for off in w_col_offsets]:
        assert off % bn == 0
        ob = off // bn
        nb = lambda p, t=t, ob=ob: jnp.minimum(p * sub_blocks + t, last_nb) + ob
        if w_transposed:
            in_specs.append(pl.BlockSpec((None, bn, slab), lambda p, i, nb=nb: (w_layer, nb(p), i)))
        elif w_layer is None:
            in_specs.append(pl.BlockSpec((slab, bn), lambda p, i, nb=nb: (i, nb(p))))
        else:
            in_specs.append(pl.BlockSpec((None, slab, bn), lambda p, i, nb=nb: (w_layer, i, nb(p))))
        operands.append(w)
    extra_bytes = 0
    if norm is not None:
        gain, ssq = norm
        if w_transposed:
            in_specs.append(pl.BlockSpec((1, slab), lambda p, i: (0, i)))
            operands.append(gain.reshape(1, k))
        else:
            in_specs.append(pl.BlockSpec((slab, HEAD_DIM), lambda p, i: (i, 0)))
            operands.append(jnp.broadcast_to(gain[:, None], (k, HEAD_DIM)))
        in_specs.append(pl.BlockSpec((bm, ssq.shape[1]), lambda p, i: (row_of(p, i), 0)))
        operands.append(ssq)
        extra_bytes += 2 * (slab * HEAD_DIM * 4 + bm * ssq.shape[1] * 4)
    for e, kind in zip(extras, extra_kinds):
        if kind == "tile":
            in_specs.append(pl.BlockSpec((bm, wide), lambda p, i: (row_of(p, i), col_of(p, i))))
            extra_bytes += 2 * bm * wide * e.dtype.itemsize
        else:
            in_specs.append(pl.BlockSpec(e.shape, lambda p, i: (0, 0)))
        operands.append(e)
    tile_spec = pl.BlockSpec((bm, wide), lambda p, i: (row_of(p, i), col_of(p, i)))
    if with_stats:
        out_specs = [tile_spec, tile_spec,
                     pl.BlockSpec((bm, HEAD_DIM), lambda p, i: (row_of(p, i), col_of(p, i)))]
        out_shape = [jax.ShapeDtypeStruct((m, n_out), out_dtype), jax.ShapeDtypeStruct((m, n_out), BF16),
                     jax.ShapeDtypeStruct((m, nj * HEAD_DIM), F32)]
        out_bytes = 2 * bm * (wide * (jnp.dtype(out_dtype).itemsize + 2) + HEAD_DIM * 4)
    else:
        out_specs = tile_spec
        out_shape = jax.ShapeDtypeStruct((m, n_out), out_dtype)
        out_bytes = 2 * bm * wide * jnp.dtype(out_dtype).itemsize
    n_ws = n_w * sub_blocks
    blk = 2 * bm * k * 2 + n_ws * (2 * slab * bn * 4 + 2 * k * bn * 2) + out_bytes + extra_bytes
    kern = functools.partial(_mm_kernel, n_w=n_w, n_sub=sub_blocks, n_extra=len(extras),
                             n_out_refs=3 if with_stats else 1, epilogue=epilogue, w_valid=w_valid,
                             w_transposed=w_transposed, slab=slab, normed=norm is not None)
    return pl.pallas_call(
        kern,
        grid=(nj + 1, ni),
        in_specs=in_specs,
        out_specs=out_specs,
        out_shape=out_shape,
        scratch_shapes=[pltpu.VMEM((2, k, bn), BF16) for _ in range(n_ws)],
        compiler_params=pltpu.CompilerParams(
            dimension_semantics=("arbitrary", "arbitrary"), vmem_limit_bytes=_vmem_limit(blk)),
        name=name,
    )(*operands)


def _gdn_kernel(q_ref, k_ref, v_ref, z_ref, ba_ref, cq_ref, ck_ref, cv_ref, alog_ref, dtb_ref,
                og_ref, o_ref, state_ref, pq_ref, pk_ref, pv_ref, *, hb, n_heads):
    n = GDN_BLOCK
    c = pl.program_id(2)

    @pl.when(c == 0)
    def _():
        state_ref[...] = jnp.zeros_like(state_ref)
        for p_ref in (pq_ref, pk_ref, pv_ref):
            p_ref[...] = jnp.zeros_like(p_ref)

    row8 = lax.broadcasted_iota(jnp.int32, (SUBLANES, q_ref.shape[1]), 0)

    def conv_silu(x_ref, p_ref, w_ref):
        x = x_ref[...]
        prev = p_ref[...]
        y = x * w_ref[GDN_CONV - 1:GDN_CONV, :]
        for j in range(GDN_CONV - 1):
            sh = GDN_CONV - 1 - j
            rolled = pltpu.roll(x, sh, 0)
            first = jnp.where(row8 < sh, pltpu.roll(prev, sh, 0), rolled[0:SUBLANES, :])
            shifted = jnp.concatenate([first, rolled[SUBLANES:, :]], axis=0)
            y = y + shifted * w_ref[j:j + 1, :]
        p_ref[...] = x[n - SUBLANES:n, :]
        return _silu(y)

    qc = conv_silu(q_ref, pq_ref, cq_ref)
    kc = conv_silu(k_ref, pk_ref, ck_ref)
    vc = conv_silu(v_ref, pv_ref, cv_ref)

    row = lax.broadcasted_iota(jnp.int32, (n, n), 0)
    col = lax.broadcasted_iota(jnp.int32, (n, n), 1)
    tril = row >= col
    strict = row > col
    eye = row == col
    eye_f = jnp.where(eye, 1.0, 0.0).astype(F32)
    ltri = jnp.where(tril, 1.0, 0.0).astype(BF16)

    ba = ba_ref[...]
    gl = ba.shape[1]
    shift = lax.rem(gl - pl.program_id(1) * hb, gl)
    beta = pltpu.roll(_sigmoid(ba), shift, 1)
    g = -jnp.exp(alog_ref[...]) * _softplus(ba + dtb_ref[...])
    gcum = pltpu.roll(_dot_exact_lhs(ltri, g), shift, 1)
    gcum_t = gcum.T

    og = og_ref[...]
    heads = range(hb)
    sls = [slice(hh * HEAD_DIM, (hh + 1) * HEAD_DIM) for hh in heads]
    qs = [qc[:, sl] for sl in sls]
    ks = [kc[:, sl] for sl in sls]
    qs = [q * (lax.rsqrt(jnp.sum(q * q, axis=-1, keepdims=True) + L2_EPS) * HEAD_DIM ** -0.5) for q in qs]
    ks = [k * lax.rsqrt(jnp.sum(k * k, axis=-1, keepdims=True) + L2_EPS) for k in ks]
    gcs = [jnp.broadcast_to(gcum[:, n_heads + hh:n_heads + hh + 1], (n, HEAD_DIM)) for hh in heads]
    bts = [jnp.broadcast_to(beta[:, hh:hh + 1], (n, HEAD_DIM)) for hh in heads]
    grows = [gcum_t[n_heads + hh:n_heads + hh + 1, :] for hh in heads]
    kbs = [k * bt for k, bt in zip(ks, bts)]
    decs = [jnp.where(tril, jnp.exp(gc - grow), 0.0) for gc, grow in zip(gcs, grows)]
    kqs = [_dot_nt(jnp.concatenate([kb, q], axis=0).astype(BF16), k.astype(BF16))
           for kb, q, k in zip(kbs, qs, ks)]
    bms = [-jnp.where(strict, kq[:n] * dec, 0.0) for kq, dec in zip(kqs, decs)]
    intras = [(kq[n:] * dec).astype(BF16) for kq, dec in zip(kqs, decs)]
    tinvs = [eye_f + bm_ for bm_ in bms]
    for _ in range(int(math.log2(n)) - 1):
        bbs = [bm_.astype(BF16) for bm_ in bms]
        bms = [_dot(bb, bb) for bb in bbs]
        tinvs = [t + _dot(t.astype(BF16), bm_.astype(BF16)) for t, bm_ in zip(tinvs, bms)]
    egs = [jnp.exp(gc) for gc in gcs]
    uws = [_dot(t.astype(BF16), jnp.concatenate([vc[:, sl] * bt, kb * eg], axis=1).astype(BF16))
           for t, sl, bt, kb, eg in zip(tinvs, sls, bts, kbs, egs)]
    ss = [state_ref[hh] for hh in heads]
    wss = [_dot(jnp.concatenate([uw[:, HEAD_DIM:], q * eg], axis=0).astype(BF16), s.astype(BF16))
           for uw, q, eg, s in zip(uws, qs, egs, ss)]
    vns = [(uw[:, :HEAD_DIM] - ws[:n]).astype(BF16) for uw, ws in zip(uws, wss)]
    glasts = [gc[n - 1:n, :] for gc in gcs]
    kds = [(k * jnp.exp(glast - gc)).astype(BF16) for k, glast, gc in zip(ks, glasts, gcs)]
    for hh in heads:
        state_ref[hh] = ss[hh] * jnp.exp(glasts[hh]) + _dot_tn(kds[hh], vns[hh])
    os_ = [ws[n:] + _dot(intra, vn) for ws, intra, vn in zip(wss, intras, vns)]
    for hh in heads:
        o = os_[hh]
        o = o * lax.rsqrt(jnp.mean(o * o, axis=-1, keepdims=True) + RMS_EPS) * og
        o_ref[:, sls[hh]] = (o * _silu(z_ref[:, sls[hh]])).astype(o_ref.dtype)


def _gdn(proj, ba, conv_w, a_log_p, dt_bias_p, out_gain, *, batch, heads, hb):
    t = proj.shape[0]
    seq = t // batch
    n = GDN_BLOCK
    assert seq % n == 0 and heads % hb == 0
    nc = seq // n
    ng = heads // hb
    wd = hb * HEAD_DIM
    gl = ba.shape[1]

    def tile(part):
        return pl.BlockSpec((n, wd), lambda b, g, c, part=part: (b * nc + c, part * ng + g))

    def cw(part):
        return pl.BlockSpec((GDN_CONV, wd), lambda b, g, c, part=part: (0, part * ng + g))

    whole = lambda shape: pl.BlockSpec(shape, lambda b, g, c: (0, 0))
    blk = 2 * (4 * n * wd * 4 + n * gl * 4 + n * wd * 2) \
        + hb * HEAD_DIM * HEAD_DIM * 4 + 3 * SUBLANES * wd * 4
    return pl.pallas_call(
        functools.partial(_gdn_kernel, hb=hb, n_heads=heads),
        grid=(batch, ng, nc),
        in_specs=[tile(0), tile(1), tile(2), tile(3),
                  pl.BlockSpec((n, gl), lambda b, g, c: (b * nc + c, 0)),
                  cw(0), cw(1), cw(2),
                  whole((1, gl)), whole((1, gl)), whole((1, HEAD_DIM))],
        out_specs=pl.BlockSpec((n, wd), lambda b, g, c: (b * nc + c, g)),
        out_shape=jax.ShapeDtypeStruct((t, heads * HEAD_DIM), BF16),
        scratch_shapes=[pltpu.VMEM((hb, HEAD_DIM, HEAD_DIM), F32)]
        + [pltpu.VMEM((SUBLANES, wd), F32)] * 3,
        compiler_params=pltpu.CompilerParams(
            dimension_semantics=("parallel", "parallel", "arbitrary"),
            vmem_limit_bytes=_vmem_limit(blk)),
        name="gdn",
    )(proj, proj, proj, proj, ba, conv_w, conv_w, conv_w, a_log_p, dt_bias_p, out_gain)


def _attn_kernel(lam_ref, q_ref, k_ref, v_ref, sg_ref, o_ref, m_ref, l_ref, acc_ref, *, blk, row_split,
                 lam_init):
    qi = pl.program_id(2)
    d = HEAD_DIM
    m_ref[...] = jnp.full_like(m_ref, -jnp.inf)
    l_ref[...] = jnp.zeros_like(l_ref)
    acc_ref[...] = jnp.zeros_like(acc_ref)

    rows = blk // row_split

    def block(j, masked):
        start = pl.multiple_of(j * blk, blk)
        chains = [(mi, r) for r in range(row_split) for mi in range(2)]

        def nkeys(r):
            return (r + 1) * rows if masked else blk

        def rsl(r):
            return slice(r * rows, (r + 1) * rows)

        ss = [_dot_nt(q_ref[rsl(r), mi * d:(mi + 1) * d],
                      k_ref[pl.ds(start, nkeys(r)), mi * d:(mi + 1) * d]) for mi, r in chains]
        if masked:
            def mask(s, r):
                qc = (r * rows + lax.broadcasted_iota(jnp.int32, s.shape, 0)) // ATTN_CHUNK
                kc = lax.broadcasted_iota(jnp.int32, s.shape, 1) // ATTN_CHUNK
                return jnp.where(kc <= qc, s, -jnp.inf)
            ss = [mask(s, r) for s, (mi, r) in zip(ss, chains)]
        m_olds = [m_ref[mi, rsl(r), :] for mi, r in chains]
        m_news = [jnp.maximum(m_old, jnp.max(s, axis=-1, keepdims=True)) for m_old, s in zip(m_olds, ss)]
        ps = [jnp.exp2(s - jnp.concatenate([m_new] * (s.shape[1] // d), axis=1))
              for s, m_new in zip(ss, m_news)]
        alphas = [jnp.exp2(m_old - m_new) for m_old, m_new in zip(m_olds, m_news)]
        for (mi, r), alpha, p, m_new in zip(chains, alphas, ps, m_news):
            l_ref[mi, rsl(r), :] = alpha * l_ref[mi, rsl(r), :] + jnp.sum(p, axis=-1, keepdims=True)
            m_ref[mi, rsl(r), :] = m_new
        pvs = [_dot(p.astype(BF16), v_ref[pl.ds(start, nkeys(r)), :]) for p, (mi, r) in zip(ps, chains)]
        for (mi, r), alpha, pv in zip(chains, alphas, pvs):
            acc_ref[mi, rsl(r), :] = acc_ref[mi, rsl(r), :] * jnp.concatenate([alpha, alpha], axis=1) + pv

    def body(j, carry):
        block(j, False)
        return carry

    lax.fori_loop(0, qi, body, 0)
    block(qi, True)

    lp = lam_ref[...]
    lam = (jnp.exp(jnp.sum(lp[0:1] * lp[1:2], axis=-1, keepdims=True))
           - jnp.exp(jnp.sum(lp[2:3] * lp[3:4], axis=-1, keepdims=True)) + lam_init)
    o = (acc_ref[0] / jnp.concatenate([l_ref[0]] * 2, axis=1)
         - lam * (acc_ref[1] / jnp.concatenate([l_ref[1]] * 2, axis=1)))
    o = o * lax.rsqrt(jnp.mean(o * o, axis=-1, keepdims=True) + RMS_EPS)
    o_ref[...] = (o * sg_ref[...] * (1.0 - lam_init)).astype(o_ref.dtype)


def _diff_attention(q, k, v, lam_params, sub_gain, *, batch, heads, lam_init, blk=1024, row_split=4):
    t = q.shape[0]
    seq = t // batch
    blk = min(blk, seq)
    assert seq % blk == 0 and blk % (row_split * ATTN_CHUNK) == 0
    nq = seq // blk
    hw = 2 * HEAD_DIM
    vmem = 2 * (2 * blk * hw * 2 + 2 * seq * hw * 2) + 4 * blk * HEAD_DIM * 4 + 2 * blk * hw * 4
    return pl.pallas_call(
        functools.partial(_attn_kernel, blk=blk, row_split=row_split, lam_init=lam_init),
        grid=(batch, heads, nq),
        in_specs=[pl.BlockSpec((4, HEAD_DIM), lambda b, h, qi: (0, 0)),
                  pl.BlockSpec((blk, hw), lambda b, h, qi: (b * nq + qi, h)),
                  pl.BlockSpec((seq, hw), lambda b, h, qi: (b, h)),
                  pl.BlockSpec((seq, hw), lambda b, h, qi: (b, h)),
                  pl.BlockSpec((1, hw), lambda b, h, qi: (0, 0))],
        out_specs=pl.BlockSpec((blk, hw), lambda b, h, qi: (b * nq + qi, h)),
        out_shape=jax.ShapeDtypeStruct((t, heads * hw), BF16),
        scratch_shapes=[pltpu.VMEM((2, blk, HEAD_DIM), F32),
                        pltpu.VMEM((2, blk, HEAD_DIM), F32),
                        pltpu.VMEM((2, blk, hw), F32)],
        compiler_params=pltpu.CompilerParams(
            dimension_semantics=("parallel", "parallel", "parallel"),
            vmem_limit_bytes=_vmem_limit(vmem)),
        name="diff_attn",
    )(lam_params, q, k, v, sub_gain)


BM = 1024
BN = 1024
BM_RESID, BN_RESID = 512, 1024
BN_FFN = 256
GDN_HEADS_PER_STEP = 16


def _normed_input(stream, gain):
    x2, xb, ssq = stream
    if xb is None:
        (hn,) = _rmsnorm(x2, gain[None, :])
        return hn, None
    return xb, (gain, ssq)


def _resid_matmul(a, w, layer, stream, last, name, bm=BM_RESID, bn=BN_RESID):
    x2 = stream[0]
    d = x2.shape[1]
    if last:
        y = _matmul(a, w, w_layer=layer, n_out=d, bm=bm, bn=bn, out_dtype=F32, epilogue=_ep_resid,
                    extras=(x2,), extra_kinds=("tile",), name=name)
        return y, None, None
    return _matmul(a, w, w_layer=layer, n_out=d, bm=bm, bn=bn, out_dtype=F32, epilogue=_ep_resid_stats,
                   extras=(x2,), extra_kinds=("tile",), with_stats=True, name=name)


def _gdn_layer(stream, i, batch, a_norm, a_w_in, a_conv, a_A_log, a_dt_bias, a_out_norm, a_w_out):
    heads = a_A_log.shape[1]
    hd = heads * HEAD_DIM
    gl = HEAD_DIM
    assert 2 * heads <= gl and a_w_in.shape[2] == 4 * hd + 2 * heads
    hb = min(GDN_HEADS_PER_STEP, heads)
    hn, norm = _normed_input(stream, a_norm[i])
    w_in_t = jnp.swapaxes(a_w_in, 1, 2)
    proj = _matmul(hn, w_in_t, w_layer=i, n_out=4 * hd, bm=BM, bn=BN, out_dtype=F32,
                   w_transposed=True, norm=norm, name="gdn_in")
    ba = _matmul(hn, w_in_t, w_layer=i, n_out=gl, bm=BM, bn=gl, out_dtype=F32, w_col_offsets=(4 * hd,),
                 w_valid=2 * heads, w_transposed=True, norm=norm, name="gdn_in_gates")
    on_a_lanes = lambda p: jnp.pad(p[i:i + 1], ((0, 0), (heads, gl - 2 * heads)))
    og = _gdn(proj, ba, a_conv[i], on_a_lanes(a_A_log), on_a_lanes(a_dt_bias), a_out_norm[i:i + 1],
              batch=batch, heads=heads, hb=hb)
    return _resid_matmul(og, a_w_out, i, stream, False, "gdn_out")


def _ffn_layer(stream, layer, last, ffn_norm, ffn_w_gate_up, ffn_w_down):
    hidden = ffn_w_down.shape[1]
    hn, norm = _normed_input(stream, ffn_norm[layer])
    h = _matmul(hn, ffn_w_gate_up, w_layer=layer, n_out=hidden, bm=BM, bn=BN_FFN, out_dtype=BF16,
                epilogue=_ep_swiglu, w_col_offsets=(0, hidden), sub_blocks=2, norm=norm,
                name="ffn_gate_up")
    return _resid_matmul(h, ffn_w_down, layer, stream, last, "ffn_down", bm=512, bn=512)


def kernel(x, a_norm, a_w_in, a_conv, a_A_log, a_dt_bias, a_out_norm, a_w_out, kv_norm, w_kv, k_norm,
           b_norm, b_w_q, b_q_norm, b_lambda, b_sub_norm, b_w_out, ffn_norm, ffn_w_gate_up, ffn_w_down):
    batch, seq, d = x.shape
    n_a = a_norm.shape[0]
    n_b = b_norm.shape[0]
    depth = n_a + n_b
    stream = (x.reshape(batch * seq, d), None, None)
    qk_width = b_w_q.shape[2]
    diff_heads = qk_width // (2 * HEAD_DIM)
    k_sh = v_sh = None
    for layer in range(depth):
        if layer < n_a:
            stream = _gdn_layer(stream, layer, batch, a_norm, a_w_in, a_conv, a_A_log, a_dt_bias,
                                a_out_norm, a_w_out)
        else:
            j = layer - n_a
            if layer == n_a:
                kvn, norm = _normed_input(stream, kv_norm)
                k_sh = _matmul(kvn, w_kv, n_out=qk_width, bm=BM, bn=BN, out_dtype=BF16,
                               epilogue=functools.partial(_ep_headnorm, scale=1.0),
                               extras=(k_norm[None, :],), extra_kinds=("row",), norm=norm, name="kv_k")
                v_sh = _matmul(kvn, w_kv, n_out=w_kv.shape[1] - qk_width, bm=BM, bn=BN, out_dtype=BF16,
                               w_col_offsets=(qk_width,), norm=norm, name="kv_v")
            hn, norm = _normed_input(stream, b_norm[j])
            lam_init = 0.8 - 0.6 * math.exp(-0.3 * layer)
            q = _matmul(hn, b_w_q, w_layer=j, n_out=qk_width, bm=BM, bn=BN, out_dtype=BF16,
                        epilogue=functools.partial(_ep_headnorm, scale=HEAD_DIM ** -0.5 * math.log2(math.e)),
                        extras=(b_q_norm[j:j + 1],), extra_kinds=("row",), norm=norm, name="attn_q")
            ao = _diff_attention(q, k_sh, v_sh, b_lambda[j], b_sub_norm[j:j + 1],
                                 batch=batch, heads=diff_heads, lam_init=lam_init)
            stream = _resid_matmul(ao, b_w_out, j, stream, False, "attn_out")
        stream = _ffn_layer(stream, layer, layer == depth - 1, ffn_norm, ffn_w_gate_up, ffn_w_down)
    return stream[0].reshape(batch, seq, d)
```

```python
import functools
import math

import jax
import jax.numpy as jnp
from jax import lax
from jax.experimental import pallas as pl
from jax.experimental.pallas import tpu as pltpu

F32 = jnp.float32
BF16 = jnp.bfloat16

RMS_EPS = 1e-6
L2_EPS = 1e-6
HEAD_DIM = 128
ATTN_CHUNK = 64
GDN_BLOCK = 128
GDN_CONV = 4
SUBLANES = 8
V7X_VMEM_CAP = 60000 * 1024


def _vmem_limit(block_bytes):
    return int(min(V7X_VMEM_CAP, block_bytes + (20 << 20)))


def _sigmoid(x):
    return 1.0 / (1.0 + jnp.exp(-x))


def _silu(x):
    return x * _sigmoid(x)


def _softplus(x):
    return jnp.maximum(x, 0.0) + jnp.log(1.0 + jnp.exp(-jnp.abs(x)))


def _dot(a, b):
    return jnp.dot(a, b, preferred_element_type=F32)


def _dot_nt(a, b):
    return lax.dot_general(a, b, (((1,), (1,)), ((), ())), preferred_element_type=F32)


def _dot_tn(a, b):
    return lax.dot_general(a, b, (((0,), (0,)), ((), ())), preferred_element_type=F32)


def _split3(x):
    hi = x.astype(BF16)
    r1 = x - hi.astype(F32)
    mid = r1.astype(BF16)
    lo = (r1 - mid.astype(F32)).astype(BF16)
    return hi, mid, lo


def _dot_exact_lhs(sel, x):
    hi, mid, lo = _split3(x)
    return _dot(sel, hi) + _dot(sel, mid) + _dot(sel, lo)


def _rmsnorm_kernel(x_ref, g_ref, *o_refs):
    x = x_ref[...]
    y = x * lax.rsqrt(jnp.mean(x * x, axis=-1, keepdims=True) + RMS_EPS)
    for n, o_ref in enumerate(o_refs):
        o_ref[...] = (y * g_ref[n:n + 1, :]).astype(o_ref.dtype)


def _rmsnorm(x, gains, rows=256):
    t, d = x.shape
    n = gains.shape[0]
    rows = min(rows, t)
    outs = pl.pallas_call(
        _rmsnorm_kernel,
        grid=(t // rows,),
        in_specs=[pl.BlockSpec((rows, d), lambda i: (i, 0)),
                  pl.BlockSpec((n, d), lambda i: (0, 0))],
        out_specs=[pl.BlockSpec((rows, d), lambda i: (i, 0))] * n,
        out_shape=[jax.ShapeDtypeStruct((t, d), BF16)] * n,
        compiler_params=pltpu.CompilerParams(
            dimension_semantics=("parallel",),
            vmem_limit_bytes=_vmem_limit(2 * rows * d * (4 + 2 * n))),
        name="rmsnorm",
    )(x, gains)
    return outs


def _ep_plain(accs, extra):
    return accs[0]


def _ep_resid(accs, extra):
    return accs[0] + extra[0][...]


def _ep_resid_stats(accs, extra):
    y = accs[0] + extra[0][...]
    ssq = jnp.sum(y * y, axis=-1, keepdims=True)
    return y, y, jnp.broadcast_to(ssq, (y.shape[0], HEAD_DIM))


def _ep_swiglu(accs, extra):
    g, u = accs
    return _silu(g) * u


def _ep_headnorm(accs, extra, scale):
    acc = accs[0]
    gain = extra[0][...] * scale
    outs = []
    for c in range(acc.shape[1] // HEAD_DIM):
        a = acc[:, c * HEAD_DIM:(c + 1) * HEAD_DIM]
        outs.append(a * lax.rsqrt(jnp.mean(a * a, axis=-1, keepdims=True) + RMS_EPS) * gain)
    return jnp.concatenate(outs, axis=1)


def _mm_kernel(*refs, n_w, n_sub, n_extra, n_out_refs, epilogue, w_valid, w_transposed, slab, normed):
    n_ws = n_w * n_sub
    pos = 1 + n_ws
    x_ref = refs[0]
    w_refs = refs[1:pos]
    if normed:
        gain_ref, ssq_ref = refs[pos:pos + 2]
        pos += 2
    extra = refs[pos:pos + n_extra]
    o_refs = refs[pos + n_extra:pos + n_extra + n_out_refs]
    wb_refs = refs[pos + n_extra + n_out_refs:pos + n_extra + n_out_refs + n_ws]
    phase = pl.program_id(0)
    i = pl.program_id(1)
    bm = x_ref.shape[0]
    if normed:
        r_ref = refs[-1]
        rows = pl.ds(pl.multiple_of(i * bm, bm), bm)

    def fill_row_scales():
        parts = ssq_ref[...]
        ssq = parts[:, 0:HEAD_DIM]
        for c in range(1, parts.shape[1] // HEAD_DIM):
            ssq = ssq + parts[:, c * HEAD_DIM:(c + 1) * HEAD_DIM]
        r_ref[rows, :] = lax.rsqrt(ssq * (1.0 / x_ref.shape[1]) + RMS_EPS)

    def fill_weights(fill):
        for w_ref, wb_ref in zip(w_refs, wb_refs):
            w = w_ref[...]
            if normed:
                g = gain_ref[...]
                if not w_transposed:
                    g = jnp.concatenate([g] * (w.shape[1] // HEAD_DIM), axis=1)
                w = w * g
            if w_valid is not None:
                n_axis = 0 if w_transposed else 1
                w = jnp.where(lax.broadcasted_iota(jnp.int32, w.shape, n_axis) < w_valid, w, 0.0)
            if w_transposed:
                w = w.T
            wb_ref[fill, pl.ds(pl.multiple_of(i * slab, slab), slab), :] = w.astype(BF16)

    def step(fill):
        fill_weights(fill)
        x = x_ref[...]
        accs = [_dot(x, wb_ref[1 - fill]) for wb_ref in wb_refs]
        if normed:
            r = r_ref[rows, :]
            accs = [a * jnp.concatenate([r] * (a.shape[1] // HEAD_DIM), axis=1) for a in accs]
        outs = [epilogue(accs[t * n_w:(t + 1) * n_w], extra) for t in range(n_sub)]
        if n_out_refs == 1:
            outs = [(o,) for o in outs]
        for k, o_ref in enumerate(o_refs):
            vals = [o[k] for o in outs]
            val = vals[0] if n_sub == 1 else jnp.concatenate(vals, axis=1)
            o_ref[...] = val.astype(o_ref.dtype)

    odd = lax.rem(phase, 2) == 1

    @pl.when(phase == 0)
    def _():
        fill_weights(0)
        if normed:
            fill_row_scales()

    pl.when(jnp.logical_and(phase >= 1, jnp.logical_not(odd)))(lambda: step(0))
    pl.when(odd)(lambda: step(1))


def _matmul(x, w, *, n_out, bm, bn, out_dtype, epilogue=_ep_plain, w_layer=None,
            w_col_offsets=(0,), extras=(), extra_kinds=(), w_valid=None, w_transposed=False,
            sub_blocks=1, norm=None, with_stats=False, name="matmul"):
    m, k = x.shape
    bm = min(bm, m)
    bn = min(bn, n_out)
    assert m % bm == 0 and n_out % bn == 0 and k % (m // bm) == 0
    wide = sub_blocks * bn
    ni, nj = m // bm, pl.cdiv(n_out, wide)
    slab = k // ni
    last_nb = n_out // bn - 1
    n_w = len(w_col_offsets)
    assert sub_blocks == 1 or ("tile" not in extra_kinds and not with_stats)
    row_of = lambda p, i: jnp.where(p == 0, 0, i)
    col_of = lambda p, i: jnp.maximum(p - 1, 0)

    in_specs = [pl.BlockSpec((bm, k), lambda p, i: (row_of(p, i), 0))]
    operands = [x]
    for t, off in [(t, off) for t in range(sub_blocks) for off in w_col_offsets]:
        assert off % bn == 0
        ob = off // bn
        nb = lambda p, t=t, ob=ob: jnp.minimum(p * sub_blocks + t, last_nb) + ob
        if w_transposed:
            in_specs.append(pl.BlockSpec((None, bn, slab), lambda p, i, nb=nb: (w_layer, nb(p), i)))
        elif w_layer is None:
            in_specs.append(pl.BlockSpec((slab, bn), lambda p, i, nb=nb: (i, nb(p))))
        else:
            in_specs.append(pl.BlockSpec((None, slab, bn), lambda p, i, nb=nb: (w_layer, i, nb(p))))
        operands.append(w)
    extra_bytes = 0
    if norm is not None:
        gain, ssq = norm
        if w_transposed:
            in_specs.append(pl.BlockSpec((1, slab), lambda p, i: (0, i)))
            operands.append(gain.reshape(1, k))
        else:
            in_specs.append(pl.BlockSpec((slab, HEAD_DIM), lambda p, i: (i, 0)))
            operands.append(jnp.broadcast_to(gain[:, None], (k, HEAD_DIM)))
        in_specs.append(pl.BlockSpec((bm, ssq.shape[1]), lambda p, i: (jnp.where(p == 0, i, ni - 1), 0)))
        operands.append(ssq)
        extra_bytes += 2 * (slab * HEAD_DIM * 4 + bm * ssq.shape[1] * 4) + m * HEAD_DIM * 4
    for e, kind in zip(extras, extra_kinds):
        if kind == "tile":
            in_specs.append(pl.BlockSpec((bm, wide), lambda p, i: (row_of(p, i), col_of(p, i))))
            extra_bytes += 2 * bm * wide * e.dtype.itemsize
        else:
            in_specs.append(pl.BlockSpec(e.shape, lambda p, i: (0, 0)))
        operands.append(e)
    tile_spec = pl.BlockSpec((bm, wide), lambda p, i: (row_of(p, i), col_of(p, i)))
    if with_stats:
        out_specs = [tile_spec, tile_spec,
                     pl.BlockSpec((bm, HEAD_DIM), lambda p, i: (row_of(p, i), col_of(p, i)))]
        out_shape = [jax.ShapeDtypeStruct((m, n_out), out_dtype), jax.ShapeDtypeStruct((m, n_out), BF16),
                     jax.ShapeDtypeStruct((m, nj * HEAD_DIM), F32)]
        out_bytes = 2 * bm * (wide * (jnp.dtype(out_dtype).itemsize + 2) + HEAD_DIM * 4)
    else:
        out_specs = tile_spec
        out_shape = jax.ShapeDtypeStruct((m, n_out), out_dtype)
        out_bytes = 2 * bm * wide * jnp.dtype(out_dtype).itemsize
    n_ws = n_w * sub_blocks
    blk = 2 * bm * k * 2 + n_ws * (2 * slab * bn * 4 + 2 * k * bn * 2) + out_bytes + extra_bytes
    kern = functools.partial(_mm_kernel, n_w=n_w, n_sub=sub_blocks, n_extra=len(extras),
                             n_out_refs=3 if with_stats else 1, epilogue=epilogue, w_valid=w_valid,
                             w_transposed=w_transposed, slab=slab, normed=norm is not None)
    return pl.pallas_call(
        kern,
        grid=(nj + 1, ni),
        in_specs=in_specs,
        out_specs=out_specs,
        out_shape=out_shape,
        scratch_shapes=[pltpu.VMEM((2, k, bn), BF16) for _ in range(n_ws)]
        + ([pltpu.VMEM((m, HEAD_DIM), F32)] if norm is not None else []),
        compiler_params=pltpu.CompilerParams(
            dimension_semantics=("arbitrary", "arbitrary"), vmem_limit_bytes=_vmem_limit(blk)),
        name=name,
    )(*operands)


def _gdn_kernel(q_ref, k_ref, v_ref, z_ref, ba_ref, cq_ref, ck_ref, cv_ref, alog_ref, dtb_ref,
                og_ref, o_ref, state_ref, pq_ref, pk_ref, pv_ref, *, hb, n_heads):
    n = GDN_BLOCK
    c = pl.program_id(2)

    @pl.when(c == 0)
    def _():
        state_ref[...] = jnp.zeros_like(state_ref)
        for p_ref in (pq_ref, pk_ref, pv_ref):
            p_ref[...] = jnp.zeros_like(p_ref)

    row8 = lax.broadcasted_iota(jnp.int32, (SUBLANES, q_ref.shape[1]), 0)

    def conv_silu(x_ref, p_ref, w_ref):
        x = x_ref[...]
        prev = p_ref[...]
        y = x * w_ref[GDN_CONV - 1:GDN_CONV, :]
        for j in range(GDN_CONV - 1):
            sh = GDN_CONV - 1 - j
            rolled = pltpu.roll(x, sh, 0)
            first = jnp.where(row8 < sh, pltpu.roll(prev, sh, 0), rolled[0:SUBLANES, :])
            shifted = jnp.concatenate([first, rolled[SUBLANES:, :]], axis=0)
            y = y + shifted * w_ref[j:j + 1, :]
        p_ref[...] = x[n - SUBLANES:n, :]
        return _silu(y)

    qc = conv_silu(q_ref, pq_ref, cq_ref)
    kc = conv_silu(k_ref, pk_ref, ck_ref)
    vc = conv_silu(v_ref, pv_ref, cv_ref)

    row = lax.broadcasted_iota(jnp.int32, (n, n), 0)
    col = lax.broadcasted_iota(jnp.int32, (n, n), 1)
    tril = row >= col
    strict = row > col
    eye = row == col
    eye_f = jnp.where(eye, 1.0, 0.0).astype(F32)
    ltri = jnp.where(tril, 1.0, 0.0).astype(BF16)

    ba = ba_ref[...]
    gl = ba.shape[1]
    shift = lax.rem(gl - pl.program_id(1) * hb, gl)
    beta = pltpu.roll(_sigmoid(ba), shift, 1)
    g = -jnp.exp(alog_ref[...]) * _softplus(ba + dtb_ref[...])
    gcum = pltpu.roll(_dot_exact_lhs(ltri, g), shift, 1)
    gcum_t = gcum.T

    og = og_ref[...]
    heads = range(hb)
    sls = [slice(hh * HEAD_DIM, (hh + 1) * HEAD_DIM) for hh in heads]
    qs = [qc[:, sl] for sl in sls]
    ks = [kc[:, sl] for sl in sls]
    qs = [q * (lax.rsqrt(jnp.sum(q * q, axis=-1, keepdims=True) + L2_EPS) * HEAD_DIM ** -0.5) for q in qs]
    ks = [k * lax.rsqrt(jnp.sum(k * k, axis=-1, keepdims=True) + L2_EPS) for k in ks]
    gcs = [jnp.broadcast_to(gcum[:, n_heads + hh:n_heads + hh + 1], (n, HEAD_DIM)) for hh in heads]
    bts = [jnp.broadcast_to(beta[:, hh:hh + 1], (n, HEAD_DIM)) for hh in heads]
    grows = [gcum_t[n_heads + hh:n_heads + hh + 1, :] for hh in heads]
    kbs = [k * bt for k, bt in zip(ks, bts)]
    decs = [jnp.where(tril, jnp.exp(gc - grow), 0.0) for gc, grow in zip(gcs, grows)]
    kqs = [_dot_nt(jnp.concatenate([kb, q], axis=0).astype(BF16), k.astype(BF16))
           for kb, q, k in zip(kbs, qs, ks)]
    bms = [-jnp.where(strict, kq[:n] * dec, 0.0) for kq, dec in zip(kqs, decs)]
    intras = [(kq[n:] * dec).astype(BF16) for kq, dec in zip(kqs, decs)]
    tinvs = [eye_f + bm_ for bm_ in bms]
    for _ in range(int(math.log2(n)) - 1):
        bbs = [bm_.astype(BF16) for bm_ in bms]
        bms = [_dot(bb, bb) for bb in bbs]
        tinvs = [t + _dot(t.astype(BF16), bm_.astype(BF16)) for t, bm_ in zip(tinvs, bms)]
    egs = [jnp.exp(gc) for gc in gcs]
    uws = [_dot(t.astype(BF16), jnp.concatenate([vc[:, sl] * bt, kb * eg], axis=1).astype(BF16))
           for t, sl, bt, kb, eg in zip(tinvs, sls, bts, kbs, egs)]
    ss = [state_ref[hh] for hh in heads]
    wss = [_dot(jnp.concatenate([uw[:, HEAD_DIM:], q * eg], axis=0).astype(BF16), s.astype(BF16))
           for uw, q, eg, s in zip(uws, qs, egs, ss)]
    vns = [(uw[:, :HEAD_DIM] - ws[:n]).astype(BF16) for uw, ws in zip(uws, wss)]
    glasts = [gc[n - 1:n, :] for gc in gcs]
    kds = [(k * jnp.exp(glast - gc)).astype(BF16) for k, glast, gc in zip(ks, glasts, gcs)]
    for hh in heads:
        state_ref[hh] = ss[hh] * jnp.exp(glasts[hh]) + _dot_tn(kds[hh], vns[hh])
    os_ = [ws[n:] + _dot(intra, vn) for ws, intra, vn in zip(wss, intras, vns)]
    for hh in heads:
        o = os_[hh]
        o = o * lax.rsqrt(jnp.mean(o * o, axis=-1, keepdims=True) + RMS_EPS) * og
        o_ref[:, sls[hh]] = (o * _silu(z_ref[:, sls[hh]])).astype(o_ref.dtype)


def _gdn(proj, ba, conv_w, a_log_p, dt_bias_p, out_gain, *, batch, heads, hb):
    t = proj.shape[0]
    seq = t // batch
    n = GDN_BLOCK
    assert seq % n == 0 and heads % hb == 0
    nc = seq // n
    ng = heads // hb
    wd = hb * HEAD_DIM
    gl = ba.shape[1]

    def tile(part):
        return pl.BlockSpec((n, wd), lambda b, g, c, part=part: (b * nc + c, part * ng + g))

    def cw(part):
        return pl.BlockSpec((GDN_CONV, wd), lambda b, g, c, part=part: (0, part * ng + g))

    whole = lambda shape: pl.BlockSpec(shape, lambda b, g, c: (0, 0))
    blk = 2 * (4 * n * wd * 4 + n * gl * 4 + n * wd * 2) \
        + hb * HEAD_DIM * HEAD_DIM * 4 + 3 * SUBLANES * wd * 4
    return pl.pallas_call(
        functools.partial(_gdn_kernel, hb=hb, n_heads=heads),
        grid=(batch, ng, nc),
        in_specs=[tile(0), tile(1), tile(2), tile(3),
                  pl.BlockSpec((n, gl), lambda b, g, c: (b * nc + c, 0)),
                  cw(0), cw(1), cw(2),
                  whole((1, gl)), whole((1, gl)), whole((1, HEAD_DIM))],
        out_specs=pl.BlockSpec((n, wd), lambda b, g, c: (b * nc + c, g)),
        out_shape=jax.ShapeDtypeStruct((t, heads * HEAD_DIM), BF16),
        scratch_shapes=[pltpu.VMEM((hb, HEAD_DIM, HEAD_DIM), F32)]
        + [pltpu.VMEM((SUBLANES, wd), F32)] * 3,
        compiler_params=pltpu.CompilerParams(
            dimension_semantics=("parallel", "parallel", "arbitrary"),
            vmem_limit_bytes=_vmem_limit(blk)),
        name="gdn",
    )(proj, proj, proj, proj, ba, conv_w, conv_w, conv_w, a_log_p, dt_bias_p, out_gain)


def _attn_kernel(lam_ref, q_ref, k_ref, v_ref, sg_ref, o_ref, m_ref, l_ref, acc_ref, *, blk, row_split,
                 lam_init):
    qi = pl.program_id(2)
    d = HEAD_DIM
    m_ref[...] = jnp.full_like(m_ref, -jnp.inf)
    l_ref[...] = jnp.zeros_like(l_ref)
    acc_ref[...] = jnp.zeros_like(acc_ref)

    rows = blk // row_split

    def block(j, masked):
        start = pl.multiple_of(j * blk, blk)
        chains = [(mi, r) for r in range(row_split) for mi in range(2)]

        def nkeys(r):
            return (r + 1) * rows if masked else blk

        def rsl(r):
            return slice(r * rows, (r + 1) * rows)

        ss = [_dot_nt(q_ref[rsl(r), mi * d:(mi + 1) * d],
                      k_ref[pl.ds(start, nkeys(r)), mi * d:(mi + 1) * d]) for mi, r in chains]
        if masked:
            def mask(s, r):
                qc = (r * rows + lax.broadcasted_iota(jnp.int32, s.shape, 0)) // ATTN_CHUNK
                kc = lax.broadcasted_iota(jnp.int32, s.shape, 1) // ATTN_CHUNK
                return jnp.where(kc <= qc, s, -jnp.inf)
            ss = [mask(s, r) for s, (mi, r) in zip(ss, chains)]
        m_olds = [m_ref[mi, rsl(r), :] for mi, r in chains]
        m_news = [jnp.maximum(m_old, jnp.max(s, axis=-1, keepdims=True)) for m_old, s in zip(m_olds, ss)]
        ps = [jnp.exp2(s - jnp.concatenate([m_new] * (s.shape[1] // d), axis=1))
              for s, m_new in zip(ss, m_news)]
        alphas = [jnp.exp2(m_old - m_new) for m_old, m_new in zip(m_olds, m_news)]
        for (mi, r), alpha, p, m_new in zip(chains, alphas, ps, m_news):
            l_ref[mi, rsl(r), :] = alpha * l_ref[mi, rsl(r), :] + jnp.sum(p, axis=-1, keepdims=True)
            m_ref[mi, rsl(r), :] = m_new
        pvs = [_dot(p.astype(BF16), v_ref[pl.ds(start, nkeys(r)), :]) for p, (mi, r) in zip(ps, chains)]
        for (mi, r), alpha, pv in zip(chains, alphas, pvs):
            acc_ref[mi, rsl(r), :] = acc_ref[mi, rsl(r), :] * jnp.concatenate([alpha, alpha], axis=1) + pv

    def body(j, carry):
        block(j, False)
        return carry

    lax.fori_loop(0, qi, body, 0)
    block(qi, True)

    lp = lam_ref[...]
    lam = (jnp.exp(jnp.sum(lp[0:1] * lp[1:2], axis=-1, keepdims=True))
           - jnp.exp(jnp.sum(lp[2:3] * lp[3:4], axis=-1, keepdims=True)) + lam_init)
    o = (acc_ref[0] / jnp.concatenate([l_ref[0]] * 2, axis=1)
         - lam * (acc_ref[1] / jnp.concatenate([l_ref[1]] * 2, axis=1)))
    o = o * lax.rsqrt(jnp.mean(o * o, axis=-1, keepdims=True) + RMS_EPS)
    o_ref[...] = (o * sg_ref[...] * (1.0 - lam_init)).astype(o_ref.dtype)


def _diff_attention(q, k, v, lam_params, sub_gain, *, batch, heads, lam_init, blk=1024, row_split=4):
    t = q.shape[0]
    seq = t // batch
    blk = min(blk, seq)
    assert seq % blk == 0 and blk % (row_split * ATTN_CHUNK) == 0
    nq = seq // blk
    hw = 2 * HEAD_DIM
    vmem = 2 * (2 * blk * hw * 2 + 2 * seq * hw * 2) + 4 * blk * HEAD_DIM * 4 + 2 * blk * hw * 4
    return pl.pallas_call(
        functools.partial(_attn_kernel, blk=blk, row_split=row_split, lam_init=lam_init),
        grid=(batch, heads, nq),
        in_specs=[pl.BlockSpec((4, HEAD_DIM), lambda b, h, qi: (0, 0)),
                  pl.BlockSpec((blk, hw), lambda b, h, qi: (b * nq + qi, h)),
                  pl.BlockSpec((seq, hw), lambda b, h, qi: (b, h)),
                  pl.BlockSpec((seq, hw), lambda b, h, qi: (b, h)),
                  pl.BlockSpec((1, hw), lambda b, h, qi: (0, 0))],
        out_specs=pl.BlockSpec((blk, hw), lambda b, h, qi: (b * nq + qi, h)),
        out_shape=jax.ShapeDtypeStruct((t, heads * hw), BF16),
        scratch_shapes=[pltpu.VMEM((2, blk, HEAD_DIM), F32),
                        pltpu.VMEM((2, blk, HEAD_DIM), F32),
                        pltpu.VMEM((2, blk, hw), F32)],
        compiler_params=pltpu.CompilerParams(
            dimension_semantics=("parallel", "parallel", "parallel"),
            vmem_limit_bytes=_vmem_limit(vmem)),
        name="diff_attn",
    )(lam_params, q, k, v, sub_gain)


BM = 1024
BN = 1024
BM_RESID, BN_RESID = 512, 1024
BM_KVQ = 512
BN_FFN = 256
GDN_HEADS_PER_STEP = 16


def _normed_input(stream, gain):
    x2, xb, ssq = stream
    if xb is None:
        (hn,) = _rmsnorm(x2, gain[None, :])
        return hn, None
    return xb, (gain, ssq)


def _resid_matmul(a, w, layer, stream, last, name, bm=BM_RESID, bn=BN_RESID):
    x2 = stream[0]
    d = x2.shape[1]
    if last:
        y = _matmul(a, w, w_layer=layer, n_out=d, bm=bm, bn=bn, out_dtype=F32, epilogue=_ep_resid,
                    extras=(x2,), extra_kinds=("tile",), name=name)
        return y, None, None
    return _matmul(a, w, w_layer=layer, n_out=d, bm=bm, bn=bn, out_dtype=F32, epilogue=_ep_resid_stats,
                   extras=(x2,), extra_kinds=("tile",), with_stats=True, name=name)


def _gdn_layer(stream, i, batch, a_norm, a_w_in, a_conv, a_A_log, a_dt_bias, a_out_norm, a_w_out):
    heads = a_A_log.shape[1]
    hd = heads * HEAD_DIM
    gl = HEAD_DIM
    assert 2 * heads <= gl and a_w_in.shape[2] == 4 * hd + 2 * heads
    hb = min(GDN_HEADS_PER_STEP, heads)
    hn, norm = _normed_input(stream, a_norm[i])
    w_in_t = jnp.swapaxes(a_w_in, 1, 2)
    proj = _matmul(hn, w_in_t, w_layer=i, n_out=4 * hd, bm=BM, bn=BN, out_dtype=F32,
                   w_transposed=True, norm=norm, name="gdn_in")
    ba = _matmul(hn, w_in_t, w_layer=i, n_out=gl, bm=BM, bn=gl, out_dtype=F32, w_col_offsets=(4 * hd,),
                 w_valid=2 * heads, w_transposed=True, norm=norm, name="gdn_in_gates")
    on_a_lanes = lambda p: jnp.pad(p[i:i + 1], ((0, 0), (heads, gl - 2 * heads)))
    og = _gdn(proj, ba, a_conv[i], on_a_lanes(a_A_log), on_a_lanes(a_dt_bias), a_out_norm[i:i + 1],
              batch=batch, heads=heads, hb=hb)
    return _resid_matmul(og, a_w_out, i, stream, False, "gdn_out")


def _ffn_layer(stream, layer, last, ffn_norm, ffn_w_gate_up, ffn_w_down):
    hidden = ffn_w_down.shape[1]
    hn, norm = _normed_input(stream, ffn_norm[layer])
    h = _matmul(hn, ffn_w_gate_up, w_layer=layer, n_out=hidden, bm=BM, bn=BN_FFN, out_dtype=BF16,
                epilogue=_ep_swiglu, w_col_offsets=(0, hidden), sub_blocks=2, norm=norm,
                name="ffn_gate_up")
    return _resid_matmul(h, ffn_w_down, layer, stream, last, "ffn_down", bm=512, bn=512)


def kernel(x, a_norm, a_w_in, a_conv, a_A_log, a_dt_bias, a_out_norm, a_w_out, kv_norm, w_kv, k_norm,
           b_norm, b_w_q, b_q_norm, b_lambda, b_sub_norm, b_w_out, ffn_norm, ffn_w_gate_up, ffn_w_down):
    batch, seq, d = x.shape
    n_a = a_norm.shape[0]
    n_b = b_norm.shape[0]
    depth = n_a + n_b
    stream = (x.reshape(batch * seq, d), None, None)
    qk_width = b_w_q.shape[2]
    diff_heads = qk_width // (2 * HEAD_DIM)
    k_sh = v_sh = None
    for layer in range(depth):
        if layer < n_a:
            stream = _gdn_layer(stream, layer, batch, a_norm, a_w_in, a_conv, a_A_log, a_dt_bias,
                                a_out_norm, a_w_out)
        else:
            j = layer - n_a
            if layer == n_a:
                kvn, norm = _normed_input(stream, kv_norm)
                k_sh = _matmul(kvn, w_kv, n_out=qk_width, bm=BM_KVQ, bn=BN, out_dtype=BF16,
                               epilogue=functools.partial(_ep_headnorm, scale=1.0),
                               extras=(k_norm[None, :],), extra_kinds=("row",), norm=norm, name="kv_k")
                v_sh = _matmul(kvn, w_kv, n_out=w_kv.shape[1] - qk_width, bm=BM_KVQ, bn=BN, out_dtype=BF16,
                               w_col_offsets=(qk_width,), norm=norm, name="kv_v")
            hn, norm = _normed_input(stream, b_norm[j])
            lam_init = 0.8 - 0.6 * math.exp(-0.3 * layer)
            q = _matmul(hn, b_w_q, w_layer=j, n_out=qk_width, bm=BM_KVQ, bn=BN, out_dtype=BF16,
                        epilogue=functools.partial(_ep_headnorm, scale=HEAD_DIM ** -0.5 * math.log2(math.e)),
                        extras=(b_q_norm[j:j + 1],), extra_kinds=("row",), norm=norm, name="attn_q")
            ao = _diff_attention(q, k_sh, v_sh, b_lambda[j], b_sub_norm[j:j + 1],
                                 batch=batch, heads=diff_heads, lam_init=lam_init)
            stream = _resid_matmul(ao, b_w_out, j, stream, False, "attn_out")
        stream = _ffn_layer(stream, layer, layer == depth - 1, ffn_norm, ffn_w_gate_up, ffn_w_down)
    return stream[0].reshape(batch, seq, d)
```

```python
import functools
import math

import jax
import jax.numpy as jnp
from jax import lax
from jax.experimental import pallas as pl
from jax.experimental.pallas import tpu as pltpu

F32 = jnp.float32
BF16 = jnp.bfloat16

RMS_EPS = 1e-6
L2_EPS = 1e-6
HEAD_DIM = 128
ATTN_CHUNK = 64
GDN_BLOCK = 128
GDN_CONV = 4
SUBLANES = 8
V7X_VMEM_CAP = 60000 * 1024


def _vmem_limit(block_bytes):
    return int(min(V7X_VMEM_CAP, block_bytes + (20 << 20)))


def _sigmoid(x):
    return 1.0 / (1.0 + jnp.exp(-x))


def _silu(x):
    return x * _sigmoid(x)


def _softplus(x):
    return jnp.maximum(x, 0.0) + jnp.log(1.0 + jnp.exp(-jnp.abs(x)))


def _dot(a, b):
    return jnp.dot(a, b, preferred_element_type=F32)


def _dot_nt(a, b):
    return lax.dot_general(a, b, (((1,), (1,)), ((), ())), preferred_element_type=F32)


def _dot_tn(a, b):
    return lax.dot_general(a, b, (((0,), (0,)), ((), ())), preferred_element_type=F32)


def _split3(x):
    hi = x.astype(BF16)
    r1 = x - hi.astype(F32)
    mid = r1.astype(BF16)
    lo = (r1 - mid.astype(F32)).astype(BF16)
    return hi, mid, lo


def _dot_exact_lhs(sel, x):
    hi, mid, lo = _split3(x)
    return _dot(sel, hi) + _dot(sel, mid) + _dot(sel, lo)


def _rmsnorm_kernel(x_ref, g_ref, *o_refs):
    x = x_ref[...]
    y = x * lax.rsqrt(jnp.mean(x * x, axis=-1, keepdims=True) + RMS_EPS)
    for n, o_ref in enumerate(o_refs):
        o_ref[...] = (y * g_ref[n:n + 1, :]).astype(o_ref.dtype)


def _rmsnorm(x, gains, rows=256):
    t, d = x.shape
    n = gains.shape[0]
    rows = min(rows, t)
    outs = pl.pallas_call(
        _rmsnorm_kernel,
        grid=(t // rows,),
        in_specs=[pl.BlockSpec((rows, d), lambda i: (i, 0)),
                  pl.BlockSpec((n, d), lambda i: (0, 0))],
        out_specs=[pl.BlockSpec((rows, d), lambda i: (i, 0))] * n,
        out_shape=[jax.ShapeDtypeStruct((t, d), BF16)] * n,
        compiler_params=pltpu.CompilerParams(
            dimension_semantics=("parallel",),
            vmem_limit_bytes=_vmem_limit(2 * rows * d * (4 + 2 * n))),
        name="rmsnorm",
    )(x, gains)
    return outs


def _ep_plain(accs, extra):
    return accs[0]


def _ep_resid(accs, extra):
    return accs[0] + extra[0][...]


def _ep_resid_stats(accs, extra):
    y = accs[0] + extra[0][...]
    ssq = jnp.sum(y * y, axis=-1, keepdims=True)
    return y, y, jnp.broadcast_to(ssq, (y.shape[0], HEAD_DIM))


def _ep_swiglu(accs, extra):
    g, u = accs
    return _silu(g) * u


def _ep_headnorm(accs, extra, scale):
    acc = accs[0]
    gain = extra[0][...] * scale
    outs = []
    for c in range(acc.shape[1] // HEAD_DIM):
        a = acc[:, c * HEAD_DIM:(c + 1) * HEAD_DIM]
        outs.append(a * lax.rsqrt(jnp.mean(a * a, axis=-1, keepdims=True) + RMS_EPS) * gain)
    return jnp.concatenate(outs, axis=1)


def _mm_kernel(*refs, n_w, n_sub, n_extra, n_out_refs, epilogue, w_valid, w_transposed, slab, normed,
               scale_table):
    n_ws = n_w * n_sub
    pos = 1 + n_ws
    x_ref = refs[0]
    w_refs = refs[1:pos]
    if normed:
        gain_ref, ssq_ref = refs[pos:pos + 2]
        pos += 2
    extra = refs[pos:pos + n_extra]
    o_refs = refs[pos + n_extra:pos + n_extra + n_out_refs]
    wb_refs = refs[pos + n_extra + n_out_refs:pos + n_extra + n_out_refs + n_ws]
    phase = pl.program_id(0)
    i = pl.program_id(1)

    def row_scales():
        parts = ssq_ref[...]
        ssq = parts[:, 0:HEAD_DIM]
        for c in range(1, parts.shape[1] // HEAD_DIM):
            ssq = ssq + parts[:, c * HEAD_DIM:(c + 1) * HEAD_DIM]
        return lax.rsqrt(ssq * (1.0 / x_ref.shape[1]) + RMS_EPS)

    if scale_table:
        r_ref = refs[-1]
        rows = pl.ds(pl.multiple_of(i * x_ref.shape[0], x_ref.shape[0]), x_ref.shape[0])

    def fill_weights(fill):
        for w_ref, wb_ref in zip(w_refs, wb_refs):
            w = w_ref[...]
            if normed:
                g = gain_ref[...]
                if not w_transposed:
                    g = jnp.concatenate([g] * (w.shape[1] // HEAD_DIM), axis=1)
                w = w * g
            if w_valid is not None:
                n_axis = 0 if w_transposed else 1
                w = jnp.where(lax.broadcasted_iota(jnp.int32, w.shape, n_axis) < w_valid, w, 0.0)
            if w_transposed:
                w = w.T
            wb_ref[fill, pl.ds(pl.multiple_of(i * slab, slab), slab), :] = w.astype(BF16)

    def step(fill):
        fill_weights(fill)
        x = x_ref[...]
        accs = [_dot(x, wb_ref[1 - fill]) for wb_ref in wb_refs]
        if normed:
            r = r_ref[rows, :] if scale_table else row_scales()
            accs =[a * jnp.concatenate([r] * (a.shape[1] // HEAD_DIM), axis=1) for a in accs]
        outs = [epilogue(accs[t * n_w:(t + 1) * n_w], extra) for t in range(n_sub)]
        if n_out_refs == 1:
            outs = [(o,) for o in outs]
        for k, o_ref in enumerate(o_refs):
            vals = [o[k] for o in outs]
            val = vals[0] if n_sub == 1 else jnp.concatenate(vals, axis=1)
            o_ref[...] = val.astype(o_ref.dtype)

    odd = lax.rem(phase, 2) == 1

    @pl.when(phase == 0)
    def _():
        fill_weights(0)
        if scale_table:
            r_ref[rows, :] = row_scales()

    pl.when(jnp.logical_and(phase >= 1, jnp.logical_not(odd)))(lambda: step(0))
    pl.when(odd)(lambda: step(1))


def _matmul(x, w, *, n_out, bm, bn, out_dtype, epilogue=_ep_plain, w_layer=None,
            w_col_offsets=(0,), extras=(), extra_kinds=(), w_valid=None, w_transposed=False,
            sub_blocks=1, norm=None, with_stats=False, name="matmul"):
    m, k = x.shape
    bm = min(bm, m)
    bn = min(bn, n_out)
    assert m % bm == 0 and n_out % bn == 0 and k % (m // bm) == 0
    wide = sub_blocks * bn
    ni, nj = m // bm, pl.cdiv(n_out, wide)
    slab = k // ni
    last_nb = n_out // bn - 1
    n_w = len(w_col_offsets)
    assert sub_blocks == 1 or ("tile" not in extra_kinds and not with_stats)
    row_of = lambda p, i: jnp.where(p == 0, 0, i)
    col_of = lambda p, i: jnp.maximum(p - 1, 0)

    in_specs = [pl.BlockSpec((bm, k), lambda p, i: (row_of(p, i), 0))]
    operands = [x]
    for t, off in [(t, off) for t in range(sub_blocks) for off in w_col_offsets]:
        assert off % bn == 0
        ob = off // bn
        nb = lambda p, t=t, ob=ob: jnp.minimum(p * sub_blocks + t, last_nb) + ob
        if w_transposed:
            in_specs.append(pl.BlockSpec((None, bn, slab), lambda p, i, nb=nb: (w_layer, nb(p), i)))
        elif w_layer is None:
            in_specs.append(pl.BlockSpec((slab, bn), lambda p, i, nb=nb: (i, nb(p))))
        else:
            in_specs.append(pl.BlockSpec((None, slab, bn), lambda p, i, nb=nb: (w_layer, i, nb(p))))
        operands.append(w)
    extra_bytes = 0
    scale_table = False
    if norm is not None:
        gain, ssq = norm
        if w_transposed:
            in_specs.append(pl.BlockSpec((1, slab), lambda p, i: (0, i)))
            operands.append(gain.reshape(1, k))
        else:
            in_specs.append(pl.BlockSpec((slab, HEAD_DIM), lambda p, i: (i, 0)))
            operands.append(jnp.broadcast_to(gain[:, None], (k, HEAD_DIM)))
        extra_bytes += 2 * (slab * HEAD_DIM * 4 + bm * ssq.shape[1] * 4)
        table_bytes = m * HEAD_DIM * 4
        fixed = 2 * bm * k * 2 + n_w * sub_blocks * (2 * slab * bn * 4 + 2 * k * bn * 2)
        results = 2 * bm * wide * (4 + jnp.dtype(out_dtype).itemsize)
        scale_table = fixed + results + extra_bytes + table_bytes <= V7X_VMEM_CAP
        if scale_table:
            extra_bytes += table_bytes
            in_specs.append(pl.BlockSpec((bm, ssq.shape[1]), lambda p, i: (jnp.where(p == 0, i, ni - 1), 0)))
        else:
            in_specs.append(pl.BlockSpec((bm, ssq.shape[1]), lambda p, i: (row_of(p, i), 0)))
        operands.append(ssq)
    for e, kind in zip(extras, extra_kinds):
        if kind == "tile":
            in_specs.append(pl.BlockSpec((bm, wide), lambda p, i: (row_of(p, i), col_of(p, i))))
            extra_bytes += 2 * bm * wide * e.dtype.itemsize
        else:
            in_specs.append(pl.BlockSpec(e.shape, lambda p, i: (0, 0)))
        operands.append(e)
    tile_spec = pl.BlockSpec((bm, wide), lambda p, i: (row_of(p, i), col_of(p, i)))
    if with_stats:
        out_specs = [tile_spec, tile_spec,
                     pl.BlockSpec((bm, HEAD_DIM), lambda p, i: (row_of(p, i), col_of(p, i)))]
        out_shape = [jax.ShapeDtypeStruct((m, n_out), out_dtype), jax.ShapeDtypeStruct((m, n_out), BF16),
                     jax.ShapeDtypeStruct((m, nj * HEAD_DIM), F32)]
        out_bytes = 2 * bm * (wide * (jnp.dtype(out_dtype).itemsize + 2) + HEAD_DIM * 4)
    else:
        out_specs = tile_spec
        out_shape = jax.ShapeDtypeStruct((m, n_out), out_dtype)
        out_bytes = 2 * bm * wide * jnp.dtype(out_dtype).itemsize
    n_ws = n_w * sub_blocks
    blk = 2 * bm * k * 2 + n_ws * (2 * slab * bn * 4 + 2 * k * bn * 2) + out_bytes + extra_bytes
    kern = functools.partial(_mm_kernel, n_w=n_w, n_sub=sub_blocks, n_extra=len(extras),
                             n_out_refs=3 if with_stats else 1, epilogue=epilogue, w_valid=w_valid,
                             w_transposed=w_transposed, slab=slab, normed=norm is not None,
                             scale_table=scale_table)
    return pl.pallas_call(
        kern,
        grid=(nj + 1, ni),
        in_specs=in_specs,
        out_specs=out_specs,
        out_shape=out_shape,
        scratch_shapes=[pltpu.VMEM((2, k, bn), BF16) for _ in range(n_ws)]
        + ([pltpu.VMEM((m, HEAD_DIM), F32)] if scale_table else []),
        compiler_params=pltpu.CompilerParams(
            dimension_semantics=("arbitrary", "arbitrary"), vmem_limit_bytes=_vmem_limit(blk)),
        name=name,
    )(*operands)


def _gdn_kernel(q_ref, k_ref, v_ref, z_ref, ba_ref, cq_ref, ck_ref, cv_ref, alog_ref, dtb_ref,
                og_ref, o_ref, state_ref, pq_ref, pk_ref, pv_ref, *, hb, n_heads):
    n = GDN_BLOCK
    c = pl.program_id(2)

    @pl.when(c == 0)
    def _():
        state_ref[...] = jnp.zeros_like(state_ref)
        for p_ref in (pq_ref, pk_ref, pv_ref):
            p_ref[...] = jnp.zeros_like(p_ref)

    row8 = lax.broadcasted_iota(jnp.int32, (SUBLANES, q_ref.shape[1]), 0)

    def conv_silu(x_ref, p_ref, w_ref):
        x = x_ref[...]
        prev = p_ref[...]
        y = x * w_ref[GDN_CONV - 1:GDN_CONV, :]
        for j in range(GDN_CONV - 1):
            sh = GDN_CONV - 1 - j
            rolled = pltpu.roll(x, sh, 0)
            first = jnp.where(row8 < sh, pltpu.roll(prev, sh, 0), rolled[0:SUBLANES, :])
            shifted = jnp.concatenate([first, rolled[SUBLANES:, :]], axis=0)
            y = y + shifted * w_ref[j:j + 1, :]
        p_ref[...] = x[n - SUBLANES:n, :]
        return _silu(y)

    qc = conv_silu(q_ref, pq_ref, cq_ref)
    kc = conv_silu(k_ref, pk_ref, ck_ref)
    vc = conv_silu(v_ref, pv_ref, cv_ref)

    row = lax.broadcasted_iota(jnp.int32, (n, n), 0)
    col = lax.broadcasted_iota(jnp.int32, (n, n), 1)
    tril = row >= col
    strict = row > col
    eye = row == col
    eye_f = jnp.where(eye, 1.0, 0.0).astype(F32)
    ltri = jnp.where(tril, 1.0, 0.0).astype(BF16)

    ba = ba_ref[...]
    gl = ba.shape[1]
    shift = lax.rem(gl - pl.program_id(1) * hb, gl)
    beta = pltpu.roll(_sigmoid(ba), shift, 1)
    g = -jnp.exp(alog_ref[...]) * _softplus(ba + dtb_ref[...])
    gcum = pltpu.roll(_dot_exact_lhs(ltri, g), shift, 1)
    gcum_t = gcum.T

    og = og_ref[...]
    heads = range(hb)
    sls = [slice(hh * HEAD_DIM, (hh + 1) * HEAD_DIM) for hh in heads]
    qs = [qc[:, sl] for sl in sls]
    ks = [kc[:, sl] for sl in sls]
    qs = [q * (lax.rsqrt(jnp.sum(q * q, axis=-1, keepdims=True) + L2_EPS) * HEAD_DIM ** -0.5) for q in qs]
    ks = [k * lax.rsqrt(jnp.sum(k * k, axis=-1, keepdims=True) + L2_EPS) for k in ks]
    gcs = [jnp.broadcast_to(gcum[:, n_heads + hh:n_heads + hh + 1], (n, HEAD_DIM)) for hh in heads]
    bts = [jnp.broadcast_to(beta[:, hh:hh + 1], (n, HEAD_DIM)) for hh in heads]
    grows = [gcum_t[n_heads + hh:n_heads + hh + 1, :] for hh in heads]
    kbs = [k * bt for k, bt in zip(ks, bts)]
    decs = [jnp.where(tril, jnp.exp(gc - grow), 0.0) for gc, grow in zip(gcs, grows)]
    kqs = [_dot_nt(jnp.concatenate([kb, q], axis=0).astype(BF16), k.astype(BF16))
           for kb, q, k in zip(kbs, qs, ks)]
    bms = [-jnp.where(strict, kq[:n] * dec, 0.0) for kq, dec in zip(kqs, decs)]
    intras = [(kq[n:] * dec).astype(BF16) for kq, dec in zip(kqs, decs)]
    tinvs = [eye_f + bm_ for bm_ in bms]
    for _ in range(int(math.log2(n)) - 1):
        bbs = [bm_.astype(BF16) for bm_ in bms]
        bms = [_dot(bb, bb) for bb in bbs]
        tinvs = [t + _dot(t.astype(BF16), bm_.astype(BF16)) for t, bm_ in zip(tinvs, bms)]
    egs = [jnp.exp(gc) for gc in gcs]
    uws = [_dot(t.astype(BF16), jnp.concatenate([vc[:, sl] * bt, kb * eg], axis=1).astype(BF16))
           for t, sl, bt, kb, eg in zip(tinvs, sls, bts, kbs, egs)]
    ss = [state_ref[hh] for hh in heads]
    wss = [_dot(jnp.concatenate([uw[:, HEAD_DIM:], q * eg], axis=0).astype(BF16), s.astype(BF16))
           for uw, q, eg, s in zip(uws, qs, egs, ss)]
    vns = [(uw[:, :HEAD_DIM] - ws[:n]).astype(BF16) for uw, ws in zip(uws, wss)]
    glasts = [gc[n - 1:n, :] for gc in gcs]
    kds = [(k * jnp.exp(glast - gc)).astype(BF16) for k, glast, gc in zip(ks, glasts, gcs)]
    for hh in heads:
        state_ref[hh] = ss[hh] * jnp.exp(glasts[hh]) + _dot_tn(kds[hh], vns[hh])
    os_ = [ws[n:] + _dot(intra, vn) for ws, intra, vn in zip(wss, intras, vns)]
    for hh in heads:
        o = os_[hh]
        o = o * lax.rsqrt(jnp.mean(o * o, axis=-1, keepdims=True) + RMS_EPS) * og
        o_ref[:, sls[hh]] = (o * _silu(z_ref[:, sls[hh]])).astype(o_ref.dtype)


def _gdn(proj, ba, conv_w, a_log_p, dt_bias_p, out_gain, *, batch, heads, hb):
    t = proj.shape[0]
    seq = t // batch
    n = GDN_BLOCK
    assert seq % n == 0 and heads % hb == 0
    nc = seq // n
    ng = heads // hb
    wd = hb * HEAD_DIM
    gl = ba.shape[1]

    def tile(part):
        return pl.BlockSpec((n, wd), lambda b, g, c, part=part: (b * nc + c, part * ng + g))

    def cw(part):
        return pl.BlockSpec((GDN_CONV, wd), lambda b, g, c, part=part: (0, part * ng + g))

    whole = lambda shape: pl.BlockSpec(shape, lambda b, g, c: (0, 0))
    blk = 2 * (4 * n * wd * 4 + n * gl * 4 + n * wd * 2) \
        + hb * HEAD_DIM * HEAD_DIM * 4 + 3 * SUBLANES * wd * 4
    return pl.pallas_call(
        functools.partial(_gdn_kernel, hb=hb, n_heads=heads),
        grid=(batch, ng, nc),
        in_specs=[tile(0), tile(1), tile(2), tile(3),
                  pl.BlockSpec((n, gl), lambda b, g, c: (b * nc + c, 0)),
                  cw(0), cw(1), cw(2),
                  whole((1, gl)), whole((1, gl)), whole((1, HEAD_DIM))],
        out_specs=pl.BlockSpec((n, wd), lambda b, g, c: (b * nc + c, g)),
        out_shape=jax.ShapeDtypeStruct((t, heads * HEAD_DIM), BF16),
        scratch_shapes=[pltpu.VMEM((hb, HEAD_DIM, HEAD_DIM), F32)]
        + [pltpu.VMEM((SUBLANES, wd), F32)] * 3,
        compiler_params=pltpu.CompilerParams(
            dimension_semantics=("parallel", "parallel", "arbitrary"),
            vmem_limit_bytes=_vmem_limit(blk)),
        name="gdn",
    )(proj, proj, proj, proj, ba, conv_w, conv_w, conv_w, a_log_p, dt_bias_p, out_gain)


def _attn_kernel(lam_ref, q_ref, k_ref, v_ref, sg_ref, o_ref, m_ref, l_ref, acc_ref, *, blk, row_split,
                 lam_init):
    qi = pl.program_id(2)
    d = HEAD_DIM
    m_ref[...] = jnp.full_like(m_ref, -jnp.inf)
    l_ref[...] = jnp.zeros_like(l_ref)
    acc_ref[...] = jnp.zeros_like(acc_ref)

    rows = blk // row_split

    def block(j, masked):
        start = pl.multiple_of(j * blk, blk)
        chains = [(mi, r) for r in range(row_split) for mi in range(2)]

        def nkeys(r):
            return (r + 1) * rows if masked else blk

        def rsl(r):
            return slice(r * rows, (r + 1) * rows)

        ss = [_dot_nt(q_ref[rsl(r), mi * d:(mi + 1) * d],
                      k_ref[pl.ds(start, nkeys(r)), mi * d:(mi + 1) * d]) for mi, r in chains]
        if masked:
            def mask(s, r):
                qc = (r * rows + lax.broadcasted_iota(jnp.int32, s.shape, 0)) // ATTN_CHUNK
                kc = lax.broadcasted_iota(jnp.int32, s.shape, 1) // ATTN_CHUNK
                return jnp.where(kc <= qc, s, -jnp.inf)
            ss = [mask(s, r) for s, (mi, r) in zip(ss, chains)]
        m_olds = [m_ref[mi, rsl(r), :] for mi, r in chains]
        m_news = [jnp.maximum(m_old, jnp.max(s, axis=-1, keepdims=True)) for m_old, s in zip(m_olds, ss)]
        ps = [jnp.exp2(s - jnp.concatenate([m_new] * (s.shape[1] // d), axis=1))
              for s, m_new in zip(ss, m_news)]
        alphas = [jnp.exp2(m_old - m_new) for m_old, m_new in zip(m_olds, m_news)]
        for (mi, r), alpha, p, m_new in zip(chains, alphas, ps, m_news):
            l_ref[mi, rsl(r), :] = alpha * l_ref[mi, rsl(r), :] + jnp.sum(p, axis=-1, keepdims=True)
            m_ref[mi, rsl(r), :] = m_new
        pvs = [_dot(p.astype(BF16), v_ref[pl.ds(start, nkeys(r)), :]) for p, (mi, r) in zip(ps, chains)]
        for (mi, r), alpha, pv in zip(chains, alphas, pvs):
            acc_ref[mi, rsl(r), :] = acc_ref[mi, rsl(r), :] * jnp.concatenate([alpha, alpha], axis=1) + pv

    def body(j, carry):
        block(j, False)
        return carry

    lax.fori_loop(0, qi, body, 0)
    block(qi, True)

    lp = lam_ref[...]
    lam = (jnp.exp(jnp.sum(lp[0:1] * lp[1:2], axis=-1, keepdims=True))
           - jnp.exp(jnp.sum(lp[2:3] * lp[3:4], axis=-1, keepdims=True)) + lam_init)
    o = (acc_ref[0] / jnp.concatenate([l_ref[0]] * 2, axis=1)
         - lam * (acc_ref[1] / jnp.concatenate([l_ref[1]] * 2, axis=1)))
    o = o * lax.rsqrt(jnp.mean(o * o, axis=-1, keepdims=True) + RMS_EPS)
    o_ref[...] = (o * sg_ref[...] * (1.0 - lam_init)).astype(o_ref.dtype)


def _diff_attention(q, k, v, lam_params, sub_gain, *, batch, heads, lam_init, blk=1024, row_split=4):
    t = q.shape[0]
    seq = t // batch
    blk = min(blk, seq)
    assert seq % blk == 0 and blk % (row_split * ATTN_CHUNK) == 0
    nq = seq // blk
    hw = 2 * HEAD_DIM
    vmem = 2 * (2 * blk * hw * 2 + 2 * seq * hw * 2) + 4 * blk * HEAD_DIM * 4 + 2 * blk * hw * 4
    return pl.pallas_call(
        functools.partial(_attn_kernel, blk=blk, row_split=row_split, lam_init=lam_init),
        grid=(batch, heads, nq),
        in_specs=[pl.BlockSpec((4, HEAD_DIM), lambda b, h, qi: (0, 0)),
                  pl.BlockSpec((blk, hw), lambda b, h, qi: (b * nq + qi, h)),
                  pl.BlockSpec((seq, hw), lambda b, h, qi: (b, h)),
                  pl.BlockSpec((seq, hw), lambda b, h, qi: (b, h)),
                  pl.BlockSpec((1, hw), lambda b, h, qi: (0, 0))],
        out_specs=pl.BlockSpec((blk, hw), lambda b, h, qi: (b * nq + qi, h)),
        out_shape=jax.ShapeDtypeStruct((t, heads * hw), BF16),
        scratch_shapes=[pltpu.VMEM((2, blk, HEAD_DIM), F32),
                        pltpu.VMEM((2, blk, HEAD_DIM), F32),
                        pltpu.VMEM((2, blk, hw), F32)],
        compiler_params=pltpu.CompilerParams(
            dimension_semantics=("parallel", "parallel", "parallel"),
            vmem_limit_bytes=_vmem_limit(vmem)),
        name="diff_attn",
    )(lam_params, q, k, v, sub_gain)


BM = 1024
BN = 1024
BM_RESID, BN_RESID = 512, 1024
BN_FFN = 256
GDN_HEADS_PER_STEP = 16


def _normed_input(stream, gain):
    x2, xb, ssq = stream
    if xb is None:
        (hn,) = _rmsnorm(x2, gain[None, :])
        return hn, None
    return xb, (gain, ssq)


def _resid_matmul(a, w, layer, stream, last, name, bm=BM_RESID, bn=BN_RESID):
    x2 = stream[0]
    d = x2.shape[1]
    if last:
        y = _matmul(a, w, w_layer=layer, n_out=d, bm=bm, bn=bn, out_dtype=F32, epilogue=_ep_resid,
                    extras=(x2,), extra_kinds=("tile",), name=name)
        return y, None, None
    return _matmul(a, w, w_layer=layer, n_out=d, bm=bm, bn=bn, out_dtype=F32, epilogue=_ep_resid_stats,
                   extras=(x2,), extra_kinds=("tile",), with_stats=True, name=name)


def _gdn_layer(stream, i, batch, a_norm, a_w_in, a_conv, a_A_log, a_dt_bias, a_out_norm, a_w_out):
    heads = a_A_log.shape[1]
    hd = heads * HEAD_DIM
    gl = HEAD_DIM
    assert 2 * heads <= gl and a_w_in.shape[2] == 4 * hd + 2 * heads
    hb = min(GDN_HEADS_PER_STEP, heads)
    hn, norm = _normed_input(stream, a_norm[i])
    w_in_t = jnp.swapaxes(a_w_in, 1, 2)
    proj = _matmul(hn, w_in_t, w_layer=i, n_out=4 * hd, bm=BM, bn=BN, out_dtype=F32,
                   w_transposed=True, norm=norm, name="gdn_in")
    ba = _matmul(hn, w_in_t, w_layer=i, n_out=gl, bm=BM, bn=gl, out_dtype=F32, w_col_offsets=(4 * hd,),
                 w_valid=2 * heads, w_transposed=True, norm=norm, name="gdn_in_gates")
    on_a_lanes = lambda p: jnp.pad(p[i:i + 1], ((0, 0), (heads, gl - 2 * heads)))
    og = _gdn(proj, ba, a_conv[i], on_a_lanes(a_A_log), on_a_lanes(a_dt_bias), a_out_norm[i:i + 1],
              batch=batch, heads=heads, hb=hb)
    return _resid_matmul(og, a_w_out, i, stream, False, "gdn_out")


def _ffn_layer(stream, layer, last, ffn_norm, ffn_w_gate_up, ffn_w_down):
    hidden = ffn_w_down.shape[1]
    hn, norm = _normed_input(stream, ffn_norm[layer])
    h = _matmul(hn, ffn_w_gate_up, w_layer=layer, n_out=hidden, bm=BM, bn=BN_FFN, out_dtype=BF16,
                epilogue=_ep_swiglu, w_col_offsets=(0, hidden), sub_blocks=2, norm=norm,
                name="ffn_gate_up")
    return _resid_matmul(h, ffn_w_down, layer, stream, last, "ffn_down", bm=512, bn=512)


def kernel(x, a_norm, a_w_in, a_conv, a_A_log, a_dt_bias, a_out_norm, a_w_out, kv_norm, w_kv, k_norm,
           b_norm, b_w_q, b_q_norm, b_lambda, b_sub_norm, b_w_out, ffn_norm, ffn_w_gate_up, ffn_w_down):
    batch, seq, d = x.shape
    n_a = a_norm.shape[0]
    n_b = b_norm.shape[0]
    depth = n_a + n_b
    stream = (x.reshape(batch * seq, d), None, None)
    qk_width = b_w_q.shape[2]
    diff_heads = qk_width // (2 * HEAD_DIM)
    k_sh = v_sh = None
    for layer in range(depth):
        if layer < n_a:
            stream = _gdn_layer(stream, layer, batch, a_norm, a_w_in, a_conv, a_A_log, a_dt_bias,
                                a_out_norm, a_w_out)
        else:
            j = layer - n_a
            if layer == n_a:
                kvn, norm = _normed_input(stream, kv_norm)
                k_sh = _matmul(kvn, w_kv, n_out=qk_width, bm=BM, bn=BN, out_dtype=BF16,
                               epilogue=functools.partial(_ep_headnorm, scale=1.0),
                               extras=(k_norm[None, :],), extra_kinds=("row",), norm=norm, name="kv_k")
                v_sh = _matmul(kvn, w_kv, n_out=w_kv.shape[1] - qk_width, bm=BM, bn=BN, out_dtype=BF16,
                               w_col_offsets=(qk_width,), norm=norm, name="kv_v")
            hn, norm = _normed_input(stream, b_norm[j])
            lam_init = 0.8 - 0.6 * math.exp(-0.3 * layer)
            q = _matmul(hn, b_w_q, w_layer=j, n_out=qk_width, bm=BM, bn=BN, out_dtype=BF16,
                        epilogue=functools.partial(_ep_headnorm, scale=HEAD_DIM ** -0.5 * math.log2(math.e)),
                        extras=(b_q_norm[j:j + 1],), extra_kinds=("row",), norm=norm, name="attn_q")
            ao = _diff_attention(q, k_sh, v_sh, b_lambda[j], b_sub_norm[j:j + 1],
                                 batch=batch, heads=diff_heads, lam_init=lam_init)
            stream = _resid_matmul(ao, b_w_out, j, stream, False, "attn_out")
        stream = _ffn_layer(stream, layer, layer == depth - 1, ffn_norm, ffn_w_gate_up, ffn_w_down)
    return stream[0].reshape(batch, seq, d)
```
